```python
import jax
import jax.numpy as jnp
from jax import lax
import numpy as np

D_MODEL = 1024
BATCH = 32
SEQ = 2048
DEPTH = 4

GRID_W = 64
CHUNK = 128
Q_BLOCK = 128
EPS = 1e-6
N_EVEN = (DEPTH + 1) // 2
N_ODD = DEPTH // 2

RET_HEADS = D_MODEL // 256
RET_DIM = 128
RET_WIDTH = RET_HEADS * RET_DIM

ATT_Q_HEADS = D_MODEL // 128
ATT_KV_HEADS = ATT_Q_HEADS // 4
ATT_DIM = 64
ATT_WIDTH = ATT_Q_HEADS * ATT_DIM
ATT_KV_WIDTH = ATT_KV_HEADS * ATT_DIM
ROPE_THETA = 10000.0

EVEN_SIZES = (RET_WIDTH, RET_WIDTH, RET_WIDTH, RET_WIDTH, ATT_WIDTH, ATT_KV_WIDTH, ATT_KV_WIDTH)
EVEN_IN = sum(EVEN_SIZES)
EVEN_SPLITS = tuple(int(s) for s in np.cumsum(EVEN_SIZES)[:-1])
MIX_WIDTH = RET_WIDTH + ATT_WIDTH

GMLP_FFN = 6 * D_MODEL
GMLP_HALF = GMLP_FFN // 2
GMLP_GROUPS = 8

D_FF = ((8 * D_MODEL // 3 + 127) // 128) * 128

kernel_name = "hybrid_retention_gqa_gmlp_macaron_encoder"


def rms_norm(x, g):
    xf = x.astype(jnp.float32)
    y = xf * lax.rsqrt(jnp.mean(xf * xf, axis=-1, keepdims=True) + EPS)
    return (y * g.astype(jnp.float32)).astype(x.dtype)


def rms_norm_plain(x):
    xf = x.astype(jnp.float32)
    return (xf * lax.rsqrt(jnp.mean(xf * xf, axis=-1, keepdims=True) + EPS)).astype(x.dtype)


def layer_norm(x, g, b):
    xf = x.astype(jnp.float32)
    mu = jnp.mean(xf, axis=-1, keepdims=True)
    xc = xf - mu
    var = jnp.mean(xc * xc, axis=-1, keepdims=True)
    return (xc * lax.rsqrt(var + EPS) * g.astype(jnp.float32) + b.astype(jnp.float32)).astype(x.dtype)


def swiglu(x, w_gate, w_up, w_down):
    return (jax.nn.silu(x @ w_gate) * (x @ w_up)) @ w_down


def axial_rope_tables(seq, head_dim):
    rows = seq // GRID_W
    row = jnp.repeat(jnp.arange(rows), GRID_W).astype(jnp.float32)
    col = jnp.tile(jnp.arange(GRID_W), rows).astype(jnp.float32)
    axis_dim = head_dim // 2
    freqs = ROPE_THETA ** (-jnp.arange(0, axis_dim, 2, dtype=jnp.float32) / axis_dim)
    ang = jnp.concatenate([row[:, None] * freqs[None, :], col[:, None] * freqs[None, :]], axis=-1)
    return jnp.cos(ang), jnp.sin(ang)


def apply_rope(x, cos, sin):
    xf = x.astype(jnp.float32).reshape(x.shape[:-1] + (x.shape[-1] // 2, 2))
    x0, x1 = xf[..., 0], xf[..., 1]
    c = cos[None, :, None, :]
    s = sin[None, :, None, :]
    out = jnp.stack([x0 * c - x1 * s, x0 * s + x1 * c], axis=-1)
    return out.reshape(x.shape).astype(x.dtype)


def chunk_states(kv, decay, reverse):
    decay = decay[None, :, None, None]

    def step(state, kv_n):
        return decay * state + kv_n, state

    _, states = lax.scan(step, jnp.zeros_like(kv[0]), kv, reverse=reverse)
    return states


def retention_bidir(q, k, v, lam_f, lam_b):
    B, S, H, d = q.shape
    N = S // CHUNK
    dt = q.dtype
    qc = q.reshape(B, N, CHUNK, H, d)
    kc = (k * d ** -0.5).reshape(B, N, CHUNK, H, d)
    vc = v.reshape(B, N, CHUNK, H, d)
    lam_f = lam_f.astype(jnp.float32)
    lam_b = lam_b.astype(jnp.float32)
    pos = jnp.arange(CHUNK, dtype=jnp.float32)
    diff = pos[:, None] - pos[None, :]
    mask = jnp.where(diff >= 0,
                     jnp.exp(-lam_f[:, None, None] * jnp.maximum(diff, 0.0)),
                     jnp.exp(-lam_b[:, None, None] * jnp.maximum(-diff, 0.0)))
    scores = jnp.einsum('bnihd,bnjhd->bnhij', qc, kc) * mask.astype(dt)
    out = jnp.einsum('bnhij,bnjhd->bnihd', scores, vc)
    kw_f = jnp.exp(-lam_f[None, :] * (CHUNK - 1.0 - pos)[:, None]).astype(dt)
    qw_f = jnp.exp(-lam_f[None, :] * (pos + 1.0)[:, None]).astype(dt)
    kw_b = jnp.exp(-lam_b[None, :] * pos[:, None]).astype(dt)
    qw_b = jnp.exp(-lam_b[None, :] * (CHUNK - pos)[:, None]).astype(dt)
    kv_f = jnp.einsum('bnjhd,bnjhe->nbhde', kc * kw_f[:, :, None], vc)
    kv_b = jnp.einsum('bnjhd,bnjhe->nbhde', kc * kw_b[:, :, None], vc)
    st_f = chunk_states(kv_f, jnp.exp(-lam_f * CHUNK).astype(dt), reverse=False)
    st_b = chunk_states(kv_b, jnp.exp(-lam_b * CHUNK).astype(dt), reverse=True)
    out = (out
           + jnp.einsum('bnihd,nbhde->bnihe', qc * qw_f[:, :, None], st_f)
           + jnp.einsum('bnihd,nbhde->bnihe', qc * qw_b[:, :, None], st_b))
    return out.reshape(B, S, H, d)


def gqa_bidir(q, k, v):
    B, S, Hq, d = q.shape
    Hkv = k.shape[2]
    G = Hq // Hkv
    nb = S // Q_BLOCK
    qb = q.reshape(B, nb, Q_BLOCK, Hkv, G, d).transpose(1, 0, 2, 3, 4, 5)
    scale = d ** -0.5

    def block(q_blk):
        s = jnp.einsum('bqhgd,bkhd->bhgqk', q_blk, k).astype(jnp.float32) * scale
        p = jax.nn.softmax(s, axis=-1).astype(v.dtype)
        return jnp.einsum('bhgqk,bkhd->bqhgd', p, v)

    o = lax.map(block, qb)
    return o.transpose(1, 0, 2, 3, 4, 5).reshape(B, S, Hq * d)


def spatial_gating(z, ln_g, ln_b, w_s, b_s):
    u, v = jnp.split(z, 2, axis=-1)
    v = layer_norm(v, ln_g, ln_b)
    B, S, Ch = v.shape
    N = S // CHUNK
    G = w_s.shape[0]
    vc = v.reshape(B, N, CHUNK, G, Ch // G)
    mixed = jnp.einsum('gij,bnjgc->bnigc', w_s, vc) + b_s.T[None, None, :, :, None]
    return u * mixed.reshape(B, S, Ch)


def setup_inputs(seed: int = 0) -> dict:
    key = jax.random.key(seed)
    ks = jax.random.split(key, 24)
    f32 = jnp.float32

    def nrm(k, shape, fan_in):
        return jax.random.normal(k, shape, f32) * fan_in ** -0.5

    def gain(k, shape):
        return 1.0 + 0.02 * jax.random.normal(k, shape, f32)

    base = jnp.log(-jnp.log(1.0 - 2.0 ** (-5.0 - jnp.arange(RET_HEADS, dtype=f32))))
    return {
        "x": jax.random.normal(ks[0], (BATCH, SEQ, D_MODEL), f32),
        "ln_ffn1": gain(ks[1], (DEPTH, D_MODEL)),
        "ffn1_w_gate": nrm(ks[2], (DEPTH, D_MODEL, D_FF), D_MODEL),
        "ffn1_w_up": nrm(ks[3], (DEPTH, D_MODEL, D_FF), D_MODEL),
        "ffn1_w_down": nrm(ks[4], (DEPTH, D_FF, D_MODEL), D_FF),
        "ln_mix": gain(ks[5], (DEPTH, D_MODEL)),
        "even_w_in": nrm(ks[6], (N_EVEN, D_MODEL, EVEN_IN), D_MODEL),
        "ret_decay_fwd": base[None, :] + 0.05 * jax.random.normal(ks[7], (N_EVEN, RET_HEADS), f32),
        "ret_decay_bwd": base[None, :] + 0.05 * jax.random.normal(ks[8], (N_EVEN, RET_HEADS), f32),
        "att_q_norm": gain(ks[9], (N_EVEN, ATT_DIM)),
        "att_k_norm": gain(ks[10], (N_EVEN, ATT_DIM)),
        "even_w_out": nrm(ks[11], (N_EVEN, MIX_WIDTH, D_MODEL), MIX_WIDTH),
        "odd_w_in": nrm(ks[12], (N_ODD, D_MODEL, GMLP_FFN), D_MODEL),
        "sgu_ln_g": gain(ks[13], (N_ODD, GMLP_HALF)),
        "sgu_ln_b": 0.02 * jax.random.normal(ks[14], (N_ODD, GMLP_HALF), f32),
        "sgu_w_s": nrm(ks[15], (N_ODD, GMLP_GROUPS, CHUNK, CHUNK), CHUNK),
        "sgu_b_s": 1.0 + 0.1 * jax.random.normal(ks[16], (N_ODD, GMLP_GROUPS, CHUNK), f32),
        "odd_w_out": nrm(ks[17], (N_ODD, GMLP_HALF, D_MODEL), GMLP_HALF),
        "ln_ffn2": gain(ks[18], (DEPTH, D_MODEL)),
        "ffn2_w_gate": nrm(ks[19], (DEPTH, D_MODEL, D_FF), D_MODEL),
        "ffn2_w_up": nrm(ks[20], (DEPTH, D_MODEL, D_FF), D_MODEL),
        "ffn2_w_down": nrm(ks[21], (DEPTH, D_FF, D_MODEL), D_FF),
        "final_norm": gain(ks[22], (D_MODEL,)),
    }


def reference(x, ln_ffn1, ffn1_w_gate, ffn1_w_up, ffn1_w_down, ln_mix, even_w_in,
              ret_decay_fwd, ret_decay_bwd, att_q_norm, att_k_norm, even_w_out,
              odd_w_in, sgu_ln_g, sgu_ln_b, sgu_w_s, sgu_b_s, odd_w_out,
              ln_ffn2, ffn2_w_gate, ffn2_w_up, ffn2_w_down, final_norm):
    B, S, _ = x.shape
    cos_r, sin_r = axial_rope_tables(S, RET_DIM)
    cos_a, sin_a = axial_rope_tables(S, ATT_DIM)
    for l in range(DEPTH):
        h = rms_norm(x, ln_ffn1[l])
        x = x + 0.5 * swiglu(h, ffn1_w_gate[l], ffn1_w_up[l], ffn1_w_down[l])
        h = rms_norm(x, ln_mix[l])
        if l % 2 == 0:
            e = l // 2
            p = h @ even_w_in[e]
            rq, rk, rv, rg, aq, ak, av = jnp.split(p, EVEN_SPLITS, axis=-1)
            rq = apply_rope(rq.reshape(B, S, RET_HEADS, RET_DIM), cos_r, sin_r)
            rk = apply_rope(rk.reshape(B, S, RET_HEADS, RET_DIM), cos_r, sin_r)
            rv = rv.reshape(B, S, RET_HEADS, RET_DIM)
            ret = retention_bidir(rq, rk, rv, jnp.exp(ret_decay_fwd[e]), jnp.exp(ret_decay_bwd[e]))
            ret = rms_norm_plain(ret).reshape(B, S, RET_WIDTH) * jax.nn.silu(rg)
            aq = rms_norm(aq.reshape(B, S, ATT_Q_HEADS, ATT_DIM), att_q_norm[e])
            ak = rms_norm(ak.reshape(B, S, ATT_KV_HEADS, ATT_DIM), att_k_norm[e])
            aq = apply_rope(aq, cos_a, sin_a)
            ak = apply_rope(ak, cos_a, sin_a)
            av = av.reshape(B, S, ATT_KV_HEADS, ATT_DIM)
            att = gqa_bidir(aq, ak, av)
            mix = jnp.concatenate([ret, att], axis=-1) @ even_w_out[e]
        else:
            o = l // 2
            z = jax.nn.gelu(h @ odd_w_in[o], approximate=False)
            mix = spatial_gating(z, sgu_ln_g[o], sgu_ln_b[o], sgu_w_s[o], sgu_b_s[o]) @ odd_w_out[o]
        x = x + mix
        h = rms_norm(x, ln_ffn2[l])
        x = x + 0.5 * swiglu(h, ffn2_w_gate[l], ffn2_w_up[l], ffn2_w_down[l])
    return rms_norm(x, final_norm)
```

```python
import functools

import numpy as np
import jax
import jax.numpy as jnp
from jax import lax
from jax.experimental import pallas as pl
from jax.experimental.pallas import tpu as pltpu

F32 = jnp.float32
BF16 = jnp.bfloat16

EPS = 1e-6
GRID_W = 64
CHUNK = 128
ROPE_THETA = 10000.0
RET_HEADS = 4
RET_DIM = 128
ATT_Q_HEADS = 8
ATT_KV_HEADS = 2
ATT_DIM = 64
ATT_GROUP = ATT_Q_HEADS // ATT_KV_HEADS
GMLP_GROUPS = 8

LANES = 128
VMEM_LIMIT_BYTES = 56 * 1024 * 1024

FFN_TM = 512
EVEN_TM = 512
ODD_TM = 256
ATT_TQ = 256
MXU_N = 256


def _params(*sem):
    return pltpu.CompilerParams(dimension_semantics=sem, vmem_limit_bytes=VMEM_LIMIT_BYTES)


def _resident(shape):
    nd = len(shape)
    return pl.BlockSpec(shape, lambda *_: (0,) * nd, pipeline_mode=pl.Buffered(1))


def _rms(x):
    return x * lax.rsqrt(jnp.mean(x * x, axis=-1, keepdims=True) + EPS)


def _sigmoid(x):
    return 1.0 / (1.0 + jnp.exp(-x))


def _dot(a, b):
    return jnp.dot(a, b, preferred_element_type=F32)


def _col_chunks(n, step):
    return tuple((c, min(c + step, n)) for c in range(0, n, step))


def _ffn_kernel(*refs, ff_chunks, final):
    if final:
        x_ref, g_ref, wg_ref, wu_ref, wd_ref, fg_ref, o_ref = refs
    else:
        x_ref, g_ref, wg_ref, wu_ref, wd_ref, o_ref = refs
    x = x_ref[...]
    hb = (_rms(x) * g_ref[...]).astype(BF16)
    acc = None
    for c0, c1 in ff_chunks:
        gate = _dot(hb, wg_ref[:, c0:c1])
        up = _dot(hb, wu_ref[:, c0:c1])
        a = (gate * _sigmoid(gate) * up).astype(BF16)
        d = _dot(a, wd_ref[c0:c1, :])
        acc = d if acc is None else acc + d
    y = x + 0.5 * acc
    if final:
        y = _rms(y) * fg_ref[...]
    o_ref[...] = y


def _ffn(x2, gain, wg, wu, wd, final_gain=None):
    t, d = x2.shape
    ff = wg.shape[1]
    tm = min(FFN_TM, t)
    final = final_gain is not None
    row = pl.BlockSpec((tm, d), lambda i: (i, 0))
    in_specs = [row, _resident((1, d)), _resident((d, ff)), _resident((d, ff)), _resident((ff, d))]
    args = [x2, gain.reshape(1, d), wg.astype(BF16), wu.astype(BF16), wd.astype(BF16)]
    if final:
        in_specs.append(_resident((1, d)))
        args.append(final_gain.reshape(1, d))
    return pl.pallas_call(
        functools.partial(_ffn_kernel, ff_chunks=_col_chunks(ff, 4 * MXU_N), final=final),
        grid=(t // tm,),
        in_specs=in_specs,
        out_specs=row,
        out_shape=jax.ShapeDtypeStruct((t, d), F32),
        compiler_params=_params("parallel"),
        name="ffn_final" if final else "ffn",
    )(*args)


def _rope(xb, cos, sin_signed, even_lane):
    nxt = pltpu.roll(xb, LANES - 1, 1)
    prv = pltpu.roll(xb, 1, 1)
    return xb * cos + jnp.where(even_lane, nxt, prv) * sin_signed


def _group_mean_sq(x, e_ref):
    sq = x * x
    hi = sq.astype(BF16)
    lo = (sq - hi.astype(F32)).astype(BF16)
    n = x.shape[1]
    e = e_ref[:n, :n]
    return _dot(hi, e) + _dot(lo, e)


def _even_in_kernel(x_ref, g_ref, w_ref, e_ref, cr_ref, sr_ref, ca_ref, sa_ref, gq_ref, gk_ref,
                    rq_ref, rk_ref, rv_ref, rg_ref, aq_ref, ak_ref, av_ref, *, splits):
    o_rq, o_rk, o_rv, o_rg, o_aq, o_ak, o_av, o_end = splits
    x = x_ref[...]
    tm = x.shape[0]
    hb = (_rms(x) * g_ref[...]).astype(BF16)
    even_lane = (lax.broadcasted_iota(jnp.int32, (tm, LANES), 1) % 2) == 0
    cr, sr = cr_ref[...], sr_ref[...]
    ca, sa = ca_ref[...], sa_ref[...]

    def proj(c0, c1):
        return _dot(hb, w_ref[:, c0:c1])

    rq = proj(o_rq, o_rk)
    rk = proj(o_rk, o_rv)
    for h in range(RET_HEADS):
        sl = slice(h * RET_DIM, (h + 1) * RET_DIM)
        rq_ref[:, sl] = _rope(rq[:, sl], cr, sr, even_lane).astype(BF16)
        rk_ref[:, sl] = (_rope(rk[:, sl], cr, sr, even_lane) * RET_DIM ** -0.5).astype(BF16)
    rv_ref[...] = proj(o_rv, o_rg).astype(BF16)
    rg_ref[...] = proj(o_rg, o_aq)

    aq = proj(o_aq, o_ak)
    aq = aq * lax.rsqrt(_group_mean_sq(aq, e_ref) + EPS) * gq_ref[...]
    for j in range(aq.shape[1] // LANES):
        sl = slice(j * LANES, (j + 1) * LANES)
        aq_ref[:, sl] = (_rope(aq[:, sl], ca, sa, even_lane) * ATT_DIM ** -0.5).astype(BF16)
    ak = proj(o_ak, o_av)
    ak = ak * lax.rsqrt(_group_mean_sq(ak, e_ref) + EPS) * gk_ref[...]
    ak = _rope(ak, ca, sa, even_lane).astype(BF16)
    av = proj(o_av, o_end).astype(BF16)
    for h in range(ATT_KV_HEADS):
        sl = slice(h * ATT_DIM, (h + 1) * ATT_DIM)
        ak_ref[h] = ak[:, sl]
        av_ref[h] = av[:, sl]


def _rope_tables(seq, head_dim):
    rows = seq // GRID_W
    row = jnp.repeat(jnp.arange(rows), GRID_W).astype(F32)
    col = jnp.tile(jnp.arange(GRID_W), rows).astype(F32)
    axis_dim = head_dim // 2
    freqs = ROPE_THETA ** (-jnp.arange(0, axis_dim, 2, dtype=F32) / axis_dim)
    ang = jnp.concatenate([row[:, None] * freqs[None, :], col[:, None] * freqs[None, :]], axis=-1)
    cos = jnp.repeat(jnp.cos(ang), 2, axis=-1)
    sign = jnp.tile(jnp.array([-1.0, 1.0], F32), head_dim // 2)
    sin = jnp.repeat(jnp.sin(ang), 2, axis=-1) * sign[None, :]
    reps = LANES // head_dim
    return jnp.tile(cos, (1, reps)), jnp.tile(sin, (1, reps))


def _even_in(x3, gain, w_in, gq, gk):
    b, s, d = x3.shape
    tm = min(EVEN_TM, s)
    ret_w = RET_HEADS * RET_DIM
    att_w = ATT_Q_HEADS * ATT_DIM
    kv_w = ATT_KV_HEADS * ATT_DIM
    sizes = (ret_w, ret_w, ret_w, ret_w, att_w, kv_w, kv_w)
    splits = (0,) + tuple(int(v) for v in np.cumsum(sizes))
    n_in = splits[-1]
    assert w_in.shape == (d, n_in)
    cr, sr = _rope_tables(s, RET_DIM)
    ca, sa = _rope_tables(s, ATT_DIM)
    grp = np.arange(att_w) // ATT_DIM
    e = jnp.asarray((grp[:, None] == grp[None, :]).astype(np.float32) / ATT_DIM, BF16)
    gq_t = jnp.tile(gq, ATT_Q_HEADS).reshape(1, att_w)
    gk_t = jnp.tile(gk, ATT_KV_HEADS).reshape(1, kv_w)

    def row(w):
        return pl.BlockSpec((None, tm, w), lambda bi, si: (bi, si, 0))

    tab = pl.BlockSpec((tm, LANES), lambda bi, si: (si, 0))
    kv_out = pl.BlockSpec((None, ATT_KV_HEADS, tm, ATT_DIM), lambda bi, si: (bi, 0, si, 0))
    return pl.pallas_call(
        functools.partial(_even_in_kernel, splits=splits),
        grid=(b, s // tm),
        in_specs=[row(d), _resident((1, d)), _resident((d, n_in)), _resident((att_w, att_w)),
                  tab, tab, tab, tab, _resident((1, att_w)), _resident((1, kv_w))],
        out_specs=[row(ret_w), row(ret_w), row(ret_w), row(ret_w), row(att_w), kv_out, kv_out],
        out_shape=[jax.ShapeDtypeStruct((b, s, ret_w), BF16),
                   jax.ShapeDtypeStruct((b, s, ret_w), BF16),
                   jax.ShapeDtypeStruct((b, s, ret_w), BF16),
                   jax.ShapeDtypeStruct((b, s, ret_w), F32),
                   jax.ShapeDtypeStruct((b, s, att_w), BF16),
                   jax.ShapeDtypeStruct((b, ATT_KV_HEADS, s, ATT_DIM), BF16),
                   jax.ShapeDtypeStruct((b, ATT_KV_HEADS, s, ATT_DIM), BF16)],
        compiler_params=_params("parallel", "parallel"),
        name="even_in",
    )(x3, gain.reshape(1, d), w_in.astype(BF16), e, cr, sr, ca, sa, gq_t, gk_t)


def _ret_kernel(df_ref, db_ref, q_ref, k_ref, v_ref, g_ref, o_ref, st_ref):
    c = CHUNK
    n_chunks = q_ref.shape[0] // c
    lam_f = jnp.exp(df_ref[...])
    lam_b = jnp.exp(db_ref[...])
    ri = lax.broadcasted_iota(jnp.int32, (c, c), 0).astype(F32)
    ci = lax.broadcasted_iota(jnp.int32, (c, c), 1).astype(F32)
    diff = ri - ci
    mask = jnp.where(diff >= 0,
                     jnp.exp(-lam_f * jnp.maximum(diff, 0.0)),
                     jnp.exp(-lam_b * jnp.maximum(-diff, 0.0)))
    kw_f = jnp.exp(-lam_f * (c - 1.0 - ri))
    qw_f = jnp.exp(-lam_f * (ri + 1.0))
    kw_b = jnp.exp(-lam_b * ri)
    qw_b = jnp.exp(-lam_b * (c - ri))
    dec_f = jnp.exp(-lam_f * c)
    dec_b = jnp.exp(-lam_b * c)
    tdot = functools.partial(lax.dot_general, dimension_numbers=(((0,), (0,)), ((), ())),
                             preferred_element_type=F32)

    def rows(n):
        return slice(n * c, (n + 1) * c)

    st = jnp.zeros((c, c), F32)
    for n in range(n_chunks):
        st_ref[n, :c, :] = st.astype(BF16)
        k = k_ref[rows(n), :].astype(F32)
        st = dec_f * st + tdot((k * kw_f).astype(BF16), v_ref[rows(n), :])
    st = jnp.zeros((c, c), F32)
    for n in reversed(range(n_chunks)):
        st_ref[n, c:, :] = st.astype(BF16)
        k = k_ref[rows(n), :].astype(F32)
        st = dec_b * st + tdot((k * kw_b).astype(BF16), v_ref[rows(n), :])

    for n in range(n_chunks):
        qb = q_ref[rows(n), :]
        q = qb.astype(F32)
        s = lax.dot_general(qb, k_ref[rows(n), :], (((1,), (1,)), ((), ())),
                            preferred_element_type=F32) * mask
        qq = jnp.concatenate([(q * qw_f).astype(BF16), (q * qw_b).astype(BF16)], axis=1)
        o = _dot(s.astype(BF16), v_ref[rows(n), :]) + _dot(qq, st_ref[n])
        gate = g_ref[rows(n), :]
        o_ref[rows(n), :] = (_rms(o) * (gate * _sigmoid(gate))).astype(BF16)


def _retention(rq, rk, rv, rg, decay_f, decay_b):
    b, s, w = rq.shape
    heads = w // RET_DIM
    n_chunks = s // CHUNK
    blk = pl.BlockSpec((None, s, RET_DIM), lambda bi, hi: (bi, 0, hi))
    dec = pl.BlockSpec((None, 1, LANES), lambda bi, hi: (hi, 0, 0))
    df = jnp.broadcast_to(decay_f.reshape(heads, 1, 1), (heads, 1, LANES))
    db = jnp.broadcast_to(decay_b.reshape(heads, 1, 1), (heads, 1, LANES))
    return pl.pallas_call(
        _ret_kernel,
        grid=(b, heads),
        in_specs=[dec, dec, blk, blk, blk, blk],
        out_specs=blk,
        out_shape=jax.ShapeDtypeStruct((b, s, w), BF16),
        scratch_shapes=[pltpu.VMEM((n_chunks, 2 * CHUNK, RET_DIM), BF16)],
        compiler_params=_params("parallel", "parallel"),
        name="retention",
    )(df, db, rq, rk, rv, rg)


def _att_kernel(q_ref, k_ref, v_ref, o_ref):
    tq = q_ref.shape[0]
    d = k_ref.shape[1]
    q = q_ref[...]
    qs = jnp.concatenate([q[:, j * d:(j + 1) * d] for j in range(ATT_GROUP)], axis=0)
    s = lax.dot_general(qs, k_ref[...], (((1,), (1,)), ((), ())), preferred_element_type=F32)
    p = jnp.exp(s - jnp.max(s, axis=-1, keepdims=True))
    denom = jnp.sum(p, axis=-1, keepdims=True)
    o = _dot(p.astype(BF16), v_ref[...]) * (1.0 / denom)
    o_ref[...] = jnp.concatenate([o[j * tq:(j + 1) * tq] for j in range(ATT_GROUP)],
                                 axis=1).astype(BF16)


def _attention(aq, ak, av):
    b, s, w = aq.shape
    tq = min(ATT_TQ, s)
    gw = ATT_GROUP * ATT_DIM
    qblk = pl.BlockSpec((None, tq, gw), lambda bi, gi, qi: (bi, qi, gi))
    kvblk = pl.BlockSpec((None, None, s, ATT_DIM), lambda bi, gi, qi: (bi, gi, 0, 0))
    return pl.pallas_call(
        _att_kernel,
        grid=(b, ATT_KV_HEADS, s // tq),
        in_specs=[qblk, kvblk, kvblk],
        out_specs=qblk,
        out_shape=jax.ShapeDtypeStruct((b, s, w), BF16),
        compiler_params=_params("parallel", "parallel", "parallel"),
        name="attention",
    )(aq, ak, av)


def _even_out_kernel(x_ref, r_ref, a_ref, w_ref, o_ref):
    rw = r_ref.shape[1]
    o_ref[...] = x_ref[...] + _dot(r_ref[...], w_ref[:rw, :]) + _dot(a_ref[...], w_ref[rw:, :])


def _even_out(x2, ret2, att2, w_out):
    t, d = x2.shape
    tm = min(EVEN_TM, t)
    rw, aw = ret2.shape[1], att2.shape[1]

    def row(w):
        return pl.BlockSpec((tm, w), lambda i: (i, 0))

    return pl.pallas_call(
        _even_out_kernel,
        grid=(t // tm,),
        in_specs=[row(d), row(rw), row(aw), _resident((rw + aw, d))],
        out_specs=row(d),
        out_shape=jax.ShapeDtypeStruct((t, d), F32),
        compiler_params=_params("parallel"),
        name="even_out",
    )(x2, ret2, att2, w_out.astype(BF16))


def _gelu(x):
    return 0.5 * x * (1.0 + lax.erf(x * (2.0 ** -0.5)))


def _odd_kernel(x_ref, g_ref, wi_ref, lg_ref, lb_ref, ws_ref, bs_ref, wo_ref, o_ref):
    x = x_ref[...]
    tm = x.shape[0]
    half = wo_ref.shape[0]
    gw = half // GMLP_GROUPS
    hb = (_rms(x) * g_ref[...]).astype(BF16)
    v = _gelu(_dot(hb, wi_ref[:, half:]))
    mu = jnp.mean(v, axis=-1, keepdims=True)
    vc = v - mu
    var = jnp.mean(vc * vc, axis=-1, keepdims=True)
    vn = (vc * lax.rsqrt(var + EPS) * lg_ref[...] + lb_ref[...]).astype(BF16)
    gated = []
    for g in range(GMLP_GROUPS):
        cols = slice(g * gw, (g + 1) * gw)
        mixed = jnp.concatenate(
            [_dot(ws_ref[g], vn[n * CHUNK:(n + 1) * CHUNK, cols]) + bs_ref[g]
             for n in range(tm // CHUNK)], axis=0)
        u = _gelu(_dot(hb, wi_ref[:, cols]))
        gated.append((u * mixed).astype(BF16))
    o_ref[...] = x + _dot(jnp.concatenate(gated, axis=1), wo_ref[...])


def _odd(x2, gain, w_in, ln_g, ln_b, w_s, b_s, w_out):
    t, d = x2.shape
    tm = min(ODD_TM, t)
    half = w_out.shape[0]
    gw = half // GMLP_GROUPS
    assert w_in.shape == (d, 2 * half) and w_s.shape == (GMLP_GROUPS, CHUNK, CHUNK)
    bs_full = jnp.broadcast_to(b_s[:, :, None], (GMLP_GROUPS, CHUNK, gw))
    row = pl.BlockSpec((tm, d), lambda i: (i, 0))
    return pl.pallas_call(
        _odd_kernel,
        grid=(t // tm,),
        in_specs=[row, _resident((1, d)), _resident((d, 2 * half)), _resident((1, half)),
                  _resident((1, half)), _resident((GMLP_GROUPS, CHUNK, CHUNK)),
                  _resident((GMLP_GROUPS, CHUNK, gw)), _resident((half, d))],
        out_specs=row,
        out_shape=jax.ShapeDtypeStruct((t, d), F32),
        compiler_params=_params("parallel"),
        name="odd_mixer",
    )(x2, gain.reshape(1, d), w_in.astype(BF16), ln_g.reshape(1, half), ln_b.reshape(1, half),
      w_s.astype(BF16), bs_full, w_out.astype(BF16))


def kernel(x, ln_ffn1, ffn1_w_gate, ffn1_w_up, ffn1_w_down, ln_mix, even_w_in, ret_decay_fwd, ret_decay_bwd, att_q_norm, att_k_norm, even_w_out, odd_w_in, sgu_ln_g, sgu_ln_b, sgu_w_s, sgu_b_s, odd_w_out, ln_ffn2, ffn2_w_gate, ffn2_w_up, ffn2_w_down, final_norm):
    b, s, d = x.shape
    depth = ln_ffn1.shape[0]
    t = b * s
    x2 = x.reshape(t, d)
    for l in range(depth):
        x2 = _ffn(x2, ln_ffn1[l], ffn1_w_gate[l], ffn1_w_up[l], ffn1_w_down[l])
        if l % 2 == 0:
            e = l // 2
            rq, rk, rv, rg, aq, ak, av = _even_in(x2.reshape(b, s, d), ln_mix[l], even_w_in[e],
                                                  att_q_norm[e], att_k_norm[e])
            ret = _retention(rq, rk, rv, rg, ret_decay_fwd[e], ret_decay_bwd[e])
            att = _attention(aq, ak, av)
            x2 = _even_out(x2, ret.reshape(t, -1), att.reshape(t, -1), even_w_out[e])
        else:
            o = l // 2
            x2 = _odd(x2, ln_mix[l], odd_w_in[o], sgu_ln_g[o], sgu_ln_b[o], sgu_w_s[o],
                      sgu_b_s[o], odd_w_out[o])
        x2 = _ffn(x2, ln_ffn2[l], ffn2_w_gate[l], ffn2_w_up[l], ffn2_w_down[l],
                  final_gain=final_norm if l == depth - 1 else None)
    return x2.reshape(b, s, d)
```

```python
import functools

import numpy as np
import jax
import jax.numpy as jnp
from jax import lax
from jax.experimental import pallas as pl
from jax.experimental.pallas import tpu as pltpu

F32 = jnp.float32
BF16 = jnp.bfloat16

EPS = 1e-6
GRID_W = 64
CHUNK = 128
ROPE_THETA = 10000.0
RET_HEADS = 4
RET_DIM = 128
ATT_Q_HEADS = 8
ATT_KV_HEADS = 2
ATT_DIM = 64
ATT_GROUP = ATT_Q_HEADS // ATT_KV_HEADS
ATT_Q_SCALE = ATT_DIM ** -0.5 * float(np.log2(np.e))
GMLP_GROUPS = 8

LANES = 128
VMEM_LIMIT_BYTES = 56 * 1024 * 1024

FFN_TM = 512
EVEN_TM = 512
ODD_TM = 256
ATT_TQ = 512
ATT_UNIT_Q = 256
ATT_LOOKAHEAD = 4
MXU_N = 256


def _params(*sem):
    return pltpu.CompilerParams(dimension_semantics=sem, vmem_limit_bytes=VMEM_LIMIT_BYTES)


def _resident(shape):
    nd = len(shape)
    return pl.BlockSpec(shape, lambda *_: (0,) * nd, pipeline_mode=pl.Buffered(1))


def _rms(x):
    return x * lax.rsqrt(jnp.mean(x * x, axis=-1, keepdims=True) + EPS)


def _sigmoid(x):
    return 1.0 / (1.0 + jnp.exp(-x))


def _dot(a, b):
    return jnp.dot(a, b, preferred_element_type=F32)


def _col_chunks(n, step):
    return tuple((c, min(c + step, n)) for c in range(0, n, step))


def _ffn_kernel(*refs, ff_chunks, final):
    if final:
        x_ref, g_ref, wg_ref, wu_ref, wd_ref, fg_ref, o_ref = refs
    else:
        x_ref, g_ref, wg_ref, wu_ref, wd_ref, o_ref = refs
    x = x_ref[...]
    hb = (_rms(x) * g_ref[...]).astype(BF16)
    acc = None
    for c0, c1 in ff_chunks:
        gate = _dot(hb, wg_ref[:, c0:c1])
        up = _dot(hb, wu_ref[:, c0:c1])
        a = (gate * _sigmoid(gate) * up).astype(BF16)
        d = _dot(a, wd_ref[c0:c1, :])
        acc = d if acc is None else acc + d
    y = x + 0.5 * acc
    if final:
        y = _rms(y) * fg_ref[...]
    o_ref[...] = y


def _ffn(x2, gain, wg, wu, wd, final_gain=None):
    t, d = x2.shape
    ff = wg.shape[1]
    tm = min(FFN_TM, t)
    final = final_gain is not None
    row = pl.BlockSpec((tm, d), lambda i: (i, 0))
    in_specs = [row, _resident((1, d)), _resident((d, ff)), _resident((d, ff)), _resident((ff, d))]
    args = [x2, gain.reshape(1, d), wg.astype(BF16), wu.astype(BF16), wd.astype(BF16)]
    if final:
        in_specs.append(_resident((1, d)))
        args.append(final_gain.reshape(1, d))
    return pl.pallas_call(
        functools.partial(_ffn_kernel, ff_chunks=_col_chunks(ff, 4 * MXU_N), final=final),
        grid=(t // tm,),
        in_specs=in_specs,
        out_specs=row,
        out_shape=jax.ShapeDtypeStruct((t, d), F32),
        compiler_params=_params("parallel"),
        name="ffn_final" if final else "ffn",
    )(*args)


def _rope(xb, cos, sin_signed, even_lane):
    nxt = pltpu.roll(xb, LANES - 1, 1)
    prv = pltpu.roll(xb, 1, 1)
    return xb * cos + jnp.where(even_lane, nxt, prv) * sin_signed


def _group_mean_sq(x, e_ref):
    sq = x * x
    hi = sq.astype(BF16)
    lo = (sq - hi.astype(F32)).astype(BF16)
    n = x.shape[1]
    e = e_ref[:n, :n]
    return _dot(hi, e) + _dot(lo, e)


def _even_in_kernel(x_ref, g_ref, w_ref, e_ref, cr_ref, sr_ref, ca_ref, sa_ref, gq_ref, gk_ref,
                    rq_ref, rk_ref, rv_ref, rg_ref, aq_ref, ak_ref, avt_ref, *, splits):
    o_rq, o_rk, o_rv, o_rg, o_aq, o_ak, o_av, o_end = splits
    x = x_ref[...]
    tm = x.shape[0]
    hb = (_rms(x) * g_ref[...]).astype(BF16)
    even_lane = (lax.broadcasted_iota(jnp.int32, (tm, LANES), 1) % 2) == 0
    cr, sr = cr_ref[...], sr_ref[...]
    ca, sa = ca_ref[...], sa_ref[...]

    def proj(c0, c1):
        return _dot(hb, w_ref[:, c0:c1])

    aq = proj(o_aq, o_ak)
    aq = aq * lax.rsqrt(_group_mean_sq(aq, e_ref) + EPS) * gq_ref[...]
    for j in range(aq.shape[1] // LANES):
        sl = slice(j * LANES, (j + 1) * LANES)
        aq_ref[:, sl] = (_rope(aq[:, sl], ca, sa, even_lane) * ATT_Q_SCALE).astype(BF16)
    ak = proj(o_ak, o_av)
    ak = ak * lax.rsqrt(_group_mean_sq(ak, e_ref) + EPS) * gk_ref[...]
    ak = _rope(ak, ca, sa, even_lane).astype(BF16)
    avt = proj(o_av, o_end).T.astype(BF16)
    for h in range(ATT_KV_HEADS):
        sl = slice(h * ATT_DIM, (h + 1) * ATT_DIM)
        ak_ref[h] = ak[:, sl]
        avt_ref[h] = avt[sl, :]

    rq = proj(o_rq, o_rk)
    rk = proj(o_rk, o_rv)
    for h in range(RET_HEADS):
        sl = slice(h * RET_DIM, (h + 1) * RET_DIM)
        rq_ref[:, sl] = _rope(rq[:, sl], cr, sr, even_lane).astype(BF16)
        rk_ref[:, sl] = (_rope(rk[:, sl], cr, sr, even_lane) * RET_DIM ** -0.5).astype(BF16)
    rv_ref[...] = proj(o_rv, o_rg).astype(BF16)
    rg_ref[...] = proj(o_rg, o_aq)


def _rope_tables(seq, head_dim):
    rows = seq // GRID_W
    row = jnp.repeat(jnp.arange(rows), GRID_W).astype(F32)
    col = jnp.tile(jnp.arange(GRID_W), rows).astype(F32)
    axis_dim = head_dim // 2
    freqs = ROPE_THETA ** (-jnp.arange(0, axis_dim, 2, dtype=F32) / axis_dim)
    ang = jnp.concatenate([row[:, None] * freqs[None, :], col[:, None] * freqs[None, :]], axis=-1)
    cos = jnp.repeat(jnp.cos(ang), 2, axis=-1)
    sign = jnp.tile(jnp.array([-1.0, 1.0], F32), head_dim // 2)
    sin = jnp.repeat(jnp.sin(ang), 2, axis=-1) * sign[None, :]
    reps = LANES // head_dim
    return jnp.tile(cos, (1, reps)), jnp.tile(sin, (1, reps))


def _even_in(x3, gain, w_in, gq, gk):
    b, s, d = x3.shape
    tm = min(EVEN_TM, s)
    ret_w = RET_HEADS * RET_DIM
    att_w = ATT_Q_HEADS * ATT_DIM
    kv_w = ATT_KV_HEADS * ATT_DIM
    sizes = (ret_w, ret_w, ret_w, ret_w, att_w, kv_w, kv_w)
    splits = (0,) + tuple(int(v) for v in np.cumsum(sizes))
    n_in = splits[-1]
    assert w_in.shape == (d, n_in)
    cr, sr = _rope_tables(s, RET_DIM)
    ca, sa = _rope_tables(s, ATT_DIM)
    grp = np.arange(att_w) // ATT_DIM
    e = jnp.asarray((grp[:, None] == grp[None, :]).astype(np.float32) / ATT_DIM, BF16)
    gq_t = jnp.tile(gq, ATT_Q_HEADS).reshape(1, att_w)
    gk_t = jnp.tile(gk, ATT_KV_HEADS).reshape(1, kv_w)

    def row(w):
        return pl.BlockSpec((None, tm, w), lambda bi, si: (bi, si, 0))

    tab = pl.BlockSpec((tm, LANES), lambda bi, si: (si, 0))
    k_out = pl.BlockSpec((None, ATT_KV_HEADS, tm, ATT_DIM), lambda bi, si: (bi, 0, si, 0))
    vt_out = pl.BlockSpec((None, ATT_KV_HEADS, ATT_DIM, tm), lambda bi, si: (bi, 0, 0, si))
    return pl.pallas_call(
        functools.partial(_even_in_kernel, splits=splits),
        grid=(b, s // tm),
        in_specs=[row(d), _resident((1, d)), _resident((d, n_in)), _resident((att_w, att_w)),
                  tab, tab, tab, tab, _resident((1, att_w)), _resident((1, kv_w))],
        out_specs=[row(ret_w), row(ret_w), row(ret_w), row(ret_w), row(att_w), k_out, vt_out],
        out_shape=[jax.ShapeDtypeStruct((b, s, ret_w), BF16),
                   jax.ShapeDtypeStruct((b, s, ret_w), BF16),
                   jax.ShapeDtypeStruct((b, s, ret_w), BF16),
                   jax.ShapeDtypeStruct((b, s, ret_w), F32),
                   jax.ShapeDtypeStruct((b, s, att_w), BF16),
                   jax.ShapeDtypeStruct((b, ATT_KV_HEADS, s, ATT_DIM), BF16),
                   jax.ShapeDtypeStruct((b, ATT_KV_HEADS, ATT_DIM, s), BF16)],
        compiler_params=_params("parallel", "parallel"),
        name="even_in",
    )(x3, gain.reshape(1, d), w_in.astype(BF16), e, cr, sr, ca, sa, gq_t, gk_t)


def _ret_kernel(df_ref, db_ref, q_ref, k_ref, v_ref, g_ref, o_ref, st_ref):
    c = CHUNK
    n_chunks = q_ref.shape[0] // c
    lam_f = jnp.exp(df_ref[...])
    lam_b = jnp.exp(db_ref[...])
    ri = lax.broadcasted_iota(jnp.int32, (c, c), 0).astype(F32)
    ci = lax.broadcasted_iota(jnp.int32, (c, c), 1).astype(F32)
    diff = ri - ci
    mask = jnp.where(diff >= 0,
                     jnp.exp(-lam_f * jnp.maximum(diff, 0.0)),
                     jnp.exp(-lam_b * jnp.maximum(-diff, 0.0)))
    kw_f = jnp.exp(-lam_f * (c - 1.0 - ri))
    qw_f = jnp.exp(-lam_f * (ri + 1.0))
    kw_b = jnp.exp(-lam_b * ri)
    qw_b = jnp.exp(-lam_b * (c - ri))
    dec_f = jnp.exp(-lam_f * c)
    dec_b = jnp.exp(-lam_b * c)
    tdot = functools.partial(lax.dot_general, dimension_numbers=(((0,), (0,)), ((), ())),
                             preferred_element_type=F32)

    def rows(n):
        return slice(n * c, (n + 1) * c)

    st = jnp.zeros((c, c), F32)
    for n in range(n_chunks):
        st_ref[n, :c, :] = st.astype(BF16)
        k = k_ref[rows(n), :].astype(F32)
        st = dec_f * st + tdot((k * kw_f).astype(BF16), v_ref[rows(n), :])
    st = jnp.zeros((c, c), F32)
    for n in reversed(range(n_chunks)):
        st_ref[n, c:, :] = st.astype(BF16)
        k = k_ref[rows(n), :].astype(F32)
        st = dec_b * st + tdot((k * kw_b).astype(BF16), v_ref[rows(n), :])

    for n in range(n_chunks):
        qb = q_ref[rows(n), :]
        q = qb.astype(F32)
        s = lax.dot_general(qb, k_ref[rows(n), :], (((1,), (1,)), ((), ())),
                            preferred_element_type=F32) * mask
        qq = jnp.concatenate([(q * qw_f).astype(BF16), (q * qw_b).astype(BF16)], axis=1)
        o = _dot(s.astype(BF16), v_ref[rows(n), :]) + _dot(qq, st_ref[n])
        gate = g_ref[rows(n), :]
        o_ref[rows(n), :] = (_rms(o) * (gate * _sigmoid(gate))).astype(BF16)


def _retention(rq, rk, rv, rg, decay_f, decay_b):
    b, s, w = rq.shape
    heads = w // RET_DIM
    n_chunks = s // CHUNK
    blk = pl.BlockSpec((None, s, RET_DIM), lambda bi, hi: (bi, 0, hi))
    dec = pl.BlockSpec((None, 1, LANES), lambda bi, hi: (hi, 0, 0))
    df = jnp.broadcast_to(decay_f.reshape(heads, 1, 1), (heads, 1, LANES))
    db = jnp.broadcast_to(decay_b.reshape(heads, 1, 1), (heads, 1, LANES))
    return pl.pallas_call(
        _ret_kernel,
        grid=(b, heads),
        in_specs=[dec, dec, blk, blk, blk, blk],
        out_specs=blk,
        out_shape=jax.ShapeDtypeStruct((b, s, w), BF16),
        scratch_shapes=[pltpu.VMEM((n_chunks, 2 * CHUNK, RET_DIM), BF16)],
        compiler_params=_params("parallel", "parallel"),
        name="retention",
    )(df, db, rq, rk, rv, rg)


def _att_kernel(q_ref, k_ref, vt_ref, o_ref, s_ref):
    d = k_ref.shape[1]
    uq = min(ATT_UNIT_Q, q_ref.shape[0])
    k = k_ref[...]
    vt = vt_ref[...]
    units = [(r, h) for r in range(q_ref.shape[0] // uq) for h in range(ATT_GROUP)]
    maxes = {}

    def scores(u):
        r, h = units[u]
        st = lax.dot_general(k, q_ref[r * uq:(r + 1) * uq, h * d:(h + 1) * d],
                             (((1,), (1,)), ((), ())), preferred_element_type=F32)
        s_ref[u] = st
        maxes[u] = jnp.max(st, axis=0, keepdims=True)

    def finish(u):
        p = jnp.exp2(s_ref[u] - maxes[u])
        denom = jnp.sum(p, axis=0, keepdims=True)
        return _dot(vt, p.astype(BF16)) * (1.0 / denom)

    for u in range(min(ATT_LOOKAHEAD, len(units))):
        scores(u)
    outs = []
    for u in range(len(units)):
        if u % 2 == 0:
            for v in range(u + ATT_LOOKAHEAD, min(u + ATT_LOOKAHEAD + 2, len(units))):
                scores(v)
        outs.append(finish(u))
        if len(outs) == ATT_GROUP:
            r = units[u][0]
            o_ref[r * uq:(r + 1) * uq, :] = jnp.concatenate(outs, axis=0).T.astype(BF16)
            outs = []


def _attention(aq, ak, avt):
    b, s, w = aq.shape
    tq = min(ATT_TQ, s)
    gw = ATT_GROUP * ATT_DIM
    qblk = pl.BlockSpec((None, tq, gw), lambda bi, gi, qi: (bi, qi, gi))
    kblk = pl.BlockSpec((None, None, s, ATT_DIM), lambda bi, gi, qi: (bi, gi, 0, 0))
    vtblk = pl.BlockSpec((None, None, ATT_DIM, s), lambda bi, gi, qi: (bi, gi, 0, 0))
    return pl.pallas_call(
        _att_kernel,
        grid=(b, ATT_KV_HEADS, s // tq),
        in_specs=[qblk, kblk, vtblk],
        out_specs=qblk,
        out_shape=jax.ShapeDtypeStruct((b, s, w), BF16),
        scratch_shapes=[pltpu.VMEM((ATT_GROUP * (tq // min(ATT_UNIT_Q, tq)), s, min(ATT_UNIT_Q, tq)),
                                   F32)],
        compiler_params=_params("parallel", "parallel", "parallel"),
        name="attention",
    )(aq, ak, avt)


def _even_out_kernel(x_ref, r_ref, a_ref, w_ref, o_ref):
    rw = r_ref.shape[1]
    o_ref[...] = x_ref[...] + _dot(r_ref[...], w_ref[:rw, :]) + _dot(a_ref[...], w_ref[rw:, :])


def _even_out(x2, ret2, att2, w_out):
    t, d = x2.shape
    tm = min(EVEN_TM, t)
    rw, aw = ret2.shape[1], att2.shape[1]

    def row(w):
        return pl.BlockSpec((tm, w), lambda i: (i, 0))

    return pl.pallas_call(
        _even_out_kernel,
        grid=(t // tm,),
        in_specs=[row(d), row(rw), row(aw), _resident((rw + aw, d))],
        out_specs=row(d),
        out_shape=jax.ShapeDtypeStruct((t, d), F32),
        compiler_params=_params("parallel"),
        name="even_out",
    )(x2, ret2, att2, w_out.astype(BF16))


def _gelu(x):
    return 0.5 * x * (1.0 + lax.erf(x * (2.0 ** -0.5)))


def _odd_kernel(x_ref, g_ref, wi_ref, lg_ref, lb_ref, ws_ref, bs_ref, wo_ref, o_ref):
    x = x_ref[...]
    tm = x.shape[0]
    half = wo_ref.shape[0]
    gw = half // GMLP_GROUPS
    hb = (_rms(x) * g_ref[...]).astype(BF16)
    v = _gelu(_dot(hb, wi_ref[:, half:]))
    mu = jnp.mean(v, axis=-1, keepdims=True)
    vc = v - mu
    var = jnp.mean(vc * vc, axis=-1, keepdims=True)
    vn = (vc * lax.rsqrt(var + EPS) * lg_ref[...] + lb_ref[...]).astype(BF16)
    gated = []
    pair = 2 * gw
    for c0 in range(0, half, pair):
        u = _gelu(_dot(hb, wi_ref[:, c0:c0 + pair]))
        mixed = jnp.concatenate(
            [jnp.concatenate(
                [_dot(ws_ref[g], vn[n * CHUNK:(n + 1) * CHUNK, g * gw:(g + 1) * gw]) + bs_ref[g]
                 for n in range(tm // CHUNK)], axis=0)
             for g in (c0 // gw, c0 // gw + 1)], axis=1)
        gated.append((u * mixed).astype(BF16))
    o_ref[...] = x + _dot(jnp.concatenate(gated, axis=1), wo_ref[...])


def _odd(x2, gain, w_in, ln_g, ln_b, w_s, b_s, w_out):
    t, d = x2.shape
    tm = min(ODD_TM, t)
    half = w_out.shape[0]
    gw = half // GMLP_GROUPS
    assert w_in.shape == (d, 2 * half) and w_s.shape == (GMLP_GROUPS, CHUNK, CHUNK)
    bs_full = jnp.broadcast_to(b_s[:, :, None], (GMLP_GROUPS, CHUNK, gw))
    row = pl.BlockSpec((tm, d), lambda i: (i, 0))
    return pl.pallas_call(
        _odd_kernel,
        grid=(t // tm,),
        in_specs=[row, _resident((1, d)), _resident((d, 2 * half)), _resident((1, half)),
                  _resident((1, half)), _resident((GMLP_GROUPS, CHUNK, CHUNK)),
                  _resident((GMLP_GROUPS, CHUNK, gw)), _resident((half, d))],
        out_specs=row,
        out_shape=jax.ShapeDtypeStruct((t, d), F32),
        compiler_params=_params("parallel"),
        name="odd_mixer",
    )(x2, gain.reshape(1, d), w_in.astype(BF16), ln_g.reshape(1, half), ln_b.reshape(1, half),
      w_s.astype(BF16), bs_full, w_out.astype(BF16))


def kernel(x, ln_ffn1, ffn1_w_gate, ffn1_w_up, ffn1_w_down, ln_mix, even_w_in, ret_decay_fwd, ret_decay_bwd, att_q_norm, att_k_norm, even_w_out, odd_w_in, sgu_ln_g, sgu_ln_b, sgu_w_s, sgu_b_s, odd_w_out, ln_ffn2, ffn2_w_gate, ffn2_w_up, ffn2_w_down, final_norm):
    b, s, d = x.shape
    depth = ln_ffn1.shape[0]
    t = b * s
    x2 = x.reshape(t, d)
    for l in range(depth):
        x2 = _ffn(x2, ln_ffn1[l], ffn1_w_gate[l], ffn1_w_up[l], ffn1_w_down[l])
        if l % 2 == 0:
            e = l // 2
            rq, rk, rv, rg, aq, ak, av = _even_in(x2.reshape(b, s, d), ln_mix[l], even_w_in[e],
                                                  att_q_norm[e], att_k_norm[e])
            ret = _retention(rq, rk, rv, rg, ret_decay_fwd[e], ret_decay_bwd[e])
            att = _attention(aq, ak, av)
            x2 = _even_out(x2, ret.reshape(t, -1), att.reshape(t, -1), even_w_out[e])
        else:
            o = l // 2
            x2 = _odd(x2, ln_mix[l], odd_w_in[o], sgu_ln_g[o], sgu_ln_b[o], sgu_w_s[o],
                      sgu_b_s[o], odd_w_out[o])
        x2 = _ffn(x2, ln_ffn2[l], ffn2_w_gate[l], ffn2_w_up[l], ffn2_w_down[l],
                  final_gain=final_norm if l == depth - 1 else None)
    return x2.reshape(b, s, d)
```

```python
import functools

import numpy as np
import jax
import jax.numpy as jnp
from jax import lax
from jax.experimental import pallas as pl
from jax.experimental.pallas import tpu as pltpu

F32 = jnp.float32
BF16 = jnp.bfloat16

EPS = 1e-6
GRID_W = 64
CHUNK = 128
ROPE_THETA = 10000.0
RET_HEADS = 4
RET_DIM = 128
ATT_Q_HEADS = 8
ATT_KV_HEADS = 2
ATT_DIM = 64
ATT_GROUP = ATT_Q_HEADS // ATT_KV_HEADS
ATT_Q_SCALE = ATT_DIM ** -0.5 * float(np.log2(np.e))
GMLP_GROUPS = 8

LANES = 128
VMEM_LIMIT_BYTES = 56 * 1024 * 1024

FFN_TM = 1024
EVEN_TM = 512
ODD_TM = 512
ATT_TQ = 512
ATT_UNIT_Q = 256
ATT_KEY_CHUNK = 512
MXU_N = 256


def _params(*sem):
    return pltpu.CompilerParams(dimension_semantics=sem, vmem_limit_bytes=VMEM_LIMIT_BYTES)


def _resident(shape):
    nd = len(shape)
    return pl.BlockSpec(shape, lambda *_: (0,) * nd, pipeline_mode=pl.Buffered(1))


def _rms(x):
    return x * lax.rsqrt(jnp.mean(x * x, axis=-1, keepdims=True) + EPS)


def _sigmoid(x):
    return 1.0 / (1.0 + jnp.exp(-x))


def _dot(a, b):
    return jnp.dot(a, b, preferred_element_type=F32)


def _col_chunks(n, step):
    return tuple((c, min(c + step, n)) for c in range(0, n, step))


def _ffn_kernel(*refs, ff_chunks, mixer_out, final):
    x_ref, g_ref, wg_ref, wu_ref, wd_ref = refs[:5]
    extra = list(refs[5:-1])
    o_ref = refs[-1]
    x = x_ref[...]
    if mixer_out:
        r_ref, a_ref, wm_ref = extra[:3]
        extra = extra[3:]
        rw = r_ref.shape[1]
        x = x + _dot(r_ref[...], wm_ref[:rw, :]) + _dot(a_ref[...], wm_ref[rw:, :])
    if final:
        fg_ref, = extra
    hb = (_rms(x) * g_ref[...]).astype(BF16)
    acc = None
    for c0, c1 in ff_chunks:
        gate = _dot(hb, wg_ref[:, c0:c1])
        up = _dot(hb, wu_ref[:, c0:c1])
        a = (gate * _sigmoid(gate) * up).astype(BF16)
        d = _dot(a, wd_ref[c0:c1, :])
        acc = d if acc is None else acc + d
    y = x + 0.5 * acc
    if final:
        y = _rms(y) * fg_ref[...]
    o_ref[...] = y


def _ffn(x2, gain, wg, wu, wd, mixer_out=None, final_gain=None):
    t, d = x2.shape
    ff = wg.shape[1]
    tm = min(FFN_TM, t)

    def row(w):
        return pl.BlockSpec((tm, w), lambda i: (i, 0))

    in_specs = [row(d), _resident((1, d)), _resident((d, ff)), _resident((d, ff)),
                _resident((ff, d))]
    args = [x2, gain.reshape(1, d), wg.astype(BF16), wu.astype(BF16), wd.astype(BF16)]
    if mixer_out is not None:
        ret2, att2, w_mix = mixer_out
        in_specs += [row(ret2.shape[1]), row(att2.shape[1]), _resident(w_mix.shape)]
        args += [ret2, att2, w_mix.astype(BF16)]
    if final_gain is not None:
        in_specs.append(_resident((1, d)))
        args.append(final_gain.reshape(1, d))
    name = "ffn" + ("_mix" if mixer_out is not None else "") + ("_final" if final_gain is not None else "")
    return pl.pallas_call(
        functools.partial(_ffn_kernel, ff_chunks=_col_chunks(ff, 4 * MXU_N),
                          mixer_out=mixer_out is not None, final=final_gain is not None),
        grid=(t // tm,),
        in_specs=in_specs,
        out_specs=row(d),
        out_shape=jax.ShapeDtypeStruct((t, d), F32),
        compiler_params=_params("parallel"),
        name=name,
    )(*args)


def _rope(xb, cos, sin_signed, even_lane):
    nxt = pltpu.roll(xb, LANES - 1, 1)
    prv = pltpu.roll(xb, 1, 1)
    return xb * cos + jnp.where(even_lane, nxt, prv) * sin_signed


def _group_mean_sq(x, e_ref):
    sq = x * x
    hi = sq.astype(BF16)
    lo = (sq - hi.astype(F32)).astype(BF16)
    n = x.shape[1]
    e = e_ref[:n, :n]
    return _dot(hi, e) + _dot(lo, e)


def _even_in_kernel(x_ref, g_ref, w_ref, e_ref, cr_ref, sr_ref, ca_ref, sa_ref, gq_ref, gk_ref,
                    rq_ref, rk_ref, rv_ref, rg_ref, aq_ref, ak_ref, avt_ref, *, splits):
    o_rq, o_rk, o_rv, o_rg, o_aq, o_ak, o_av, o_end = splits
    x = x_ref[...]
    tm = x.shape[0]
    hb = (_rms(x) * g_ref[...]).astype(BF16)
    even_lane = (lax.broadcasted_iota(jnp.int32, (tm, LANES), 1) % 2) == 0
    cr, sr = cr_ref[...], sr_ref[...]
    ca, sa = ca_ref[...], sa_ref[...]

    def proj(c0, c1):
        return _dot(hb, w_ref[:, c0:c1])

    aq = proj(o_aq, o_ak)
    aq = aq * lax.rsqrt(_group_mean_sq(aq, e_ref) + EPS) * gq_ref[...]
    for j in range(aq.shape[1] // LANES):
        sl = slice(j * LANES, (j + 1) * LANES)
        aq_ref[:, sl] = (_rope(aq[:, sl], ca, sa, even_lane) * ATT_Q_SCALE).astype(BF16)
    ak = proj(o_ak, o_av)
    ak = ak * lax.rsqrt(_group_mean_sq(ak, e_ref) + EPS) * gk_ref[...]
    ak = _rope(ak, ca, sa, even_lane).astype(BF16)
    avt = proj(o_av, o_end).T.astype(BF16)
    for h in range(ATT_KV_HEADS):
        sl = slice(h * ATT_DIM, (h + 1) * ATT_DIM)
        ak_ref[h] = ak[:, sl]
        avt_ref[h] = avt[sl, :]

    rq = proj(o_rq, o_rk)
    rk = proj(o_rk, o_rv)
    for h in range(RET_HEADS):
        sl = slice(h * RET_DIM, (h + 1) * RET_DIM)
        rq_ref[:, sl] = _rope(rq[:, sl], cr, sr, even_lane).astype(BF16)
        rk_ref[:, sl] = (_rope(rk[:, sl], cr, sr, even_lane) * RET_DIM ** -0.5).astype(BF16)
    rv_ref[...] = proj(o_rv, o_rg).astype(BF16)
    rg_ref[...] = proj(o_rg, o_aq)


def _rope_tables(seq, head_dim):
    rows = seq // GRID_W
    row = jnp.repeat(jnp.arange(rows), GRID_W).astype(F32)
    col = jnp.tile(jnp.arange(GRID_W), rows).astype(F32)
    axis_dim = head_dim // 2
    freqs = ROPE_THETA ** (-jnp.arange(0, axis_dim, 2, dtype=F32) / axis_dim)
    ang = jnp.concatenate([row[:, None] * freqs[None, :], col[:, None] * freqs[None, :]], axis=-1)
    cos = jnp.repeat(jnp.cos(ang), 2, axis=-1)
    sign = jnp.tile(jnp.array([-1.0, 1.0], F32), head_dim // 2)
    sin = jnp.repeat(jnp.sin(ang), 2, axis=-1) * sign[None, :]
    reps = LANES // head_dim
    return jnp.tile(cos, (1, reps)), jnp.tile(sin, (1, reps))


def _even_in(x3, gain, w_in, gq, gk):
    b, s, d = x3.shape
    tm = min(EVEN_TM, s)
    ret_w = RET_HEADS * RET_DIM
    att_w = ATT_Q_HEADS * ATT_DIM
    kv_w = ATT_KV_HEADS * ATT_DIM
    sizes = (ret_w, ret_w, ret_w, ret_w, att_w, kv_w, kv_w)
    splits = (0,) + tuple(int(v) for v in np.cumsum(sizes))
    n_in = splits[-1]
    assert w_in.shape == (d, n_in)
    cr, sr = _rope_tables(s, RET_DIM)
    ca, sa = _rope_tables(s, ATT_DIM)
    grp = np.arange(att_w) // ATT_DIM
    e = jnp.asarray((grp[:, None] == grp[None, :]).astype(np.float32) / ATT_DIM, BF16)
    gq_t = jnp.tile(gq, ATT_Q_HEADS).reshape(1, att_w)
    gk_t = jnp.tile(gk, ATT_KV_HEADS).reshape(1, kv_w)

    def row(w):
        return pl.BlockSpec((None, tm, w), lambda bi, si: (bi, si, 0))

    tab = pl.BlockSpec((tm, LANES), lambda bi, si: (si, 0))
    k_out = pl.BlockSpec((None, ATT_KV_HEADS, tm, ATT_DIM), lambda bi, si: (bi, 0, si, 0))
    vt_out = pl.BlockSpec((None, ATT_KV_HEADS, ATT_DIM, tm), lambda bi, si: (bi, 0, 0, si))
    return pl.pallas_call(
        functools.partial(_even_in_kernel, splits=splits),
        grid=(b, s // tm),
        in_specs=[row(d), _resident((1, d)), _resident((d, n_in)), _resident((att_w, att_w)),
                  tab, tab, tab, tab, _resident((1, att_w)), _resident((1, kv_w))],
        out_specs=[row(ret_w), row(ret_w), row(ret_w), row(ret_w), row(att_w), k_out, vt_out],
        out_shape=[jax.ShapeDtypeStruct((b, s, ret_w), BF16),
                   jax.ShapeDtypeStruct((b, s, ret_w), BF16),
                   jax.ShapeDtypeStruct((b, s, ret_w), BF16),
                   jax.ShapeDtypeStruct((b, s, ret_w), F32),
                   jax.ShapeDtypeStruct((b, s, att_w), BF16),
                   jax.ShapeDtypeStruct((b, ATT_KV_HEADS, s, ATT_DIM), BF16),
                   jax.ShapeDtypeStruct((b, ATT_KV_HEADS, ATT_DIM, s), BF16)],
        compiler_params=_params("parallel", "parallel"),
        name="even_in",
    )(x3, gain.reshape(1, d), w_in.astype(BF16), e, cr, sr, ca, sa, gq_t, gk_t)


def _ret_kernel(df_ref, db_ref, q_ref, k_ref, v_ref, g_ref, o_ref, st_ref):
    c = CHUNK
    n_chunks = q_ref.shape[0] // c
    lam_f = jnp.exp(df_ref[...])
    lam_b = jnp.exp(db_ref[...])
    ri = lax.broadcasted_iota(jnp.int32, (c, c), 0).astype(F32)
    ci = lax.broadcasted_iota(jnp.int32, (c, c), 1).astype(F32)
    diff = ri - ci
    mask = jnp.where(diff >= 0,
                     jnp.exp(-lam_f * jnp.maximum(diff, 0.0)),
                     jnp.exp(-lam_b * jnp.maximum(-diff, 0.0)))
    kw_f = jnp.exp(-lam_f * (c - 1.0 - ri))
    qw_f = jnp.exp(-lam_f * (ri + 1.0))
    kw_b = jnp.exp(-lam_b * ri)
    qw_b = jnp.exp(-lam_b * (c - ri))
    dec_f = jnp.exp(-lam_f * c)
    dec_b = jnp.exp(-lam_b * c)
    tdot = functools.partial(lax.dot_general, dimension_numbers=(((0,), (0,)), ((), ())),
                             preferred_element_type=F32)

    def rows(n):
        return slice(n * c, (n + 1) * c)

    st = jnp.zeros((c, c), F32)
    for n in range(n_chunks):
        st_ref[n, :c, :] = st.astype(BF16)
        k = k_ref[rows(n), :].astype(F32)
        st = dec_f * st + tdot((k * kw_f).astype(BF16), v_ref[rows(n), :])
    st = jnp.zeros((c, c), F32)
    for n in reversed(range(n_chunks)):
        st_ref[n, c:, :] = st.astype(BF16)
        k = k_ref[rows(n), :].astype(F32)
        st = dec_b * st + tdot((k * kw_b).astype(BF16), v_ref[rows(n), :])

    for n in range(n_chunks):
        qb = q_ref[rows(n), :]
        q = qb.astype(F32)
        s = lax.dot_general(qb, k_ref[rows(n), :], (((1,), (1,)), ((), ())),
                            preferred_element_type=F32) * mask
        qq = jnp.concatenate([(q * qw_f).astype(BF16), (q * qw_b).astype(BF16)], axis=1)
        o = _dot(s.astype(BF16), v_ref[rows(n), :]) + _dot(qq, st_ref[n])
        gate = g_ref[rows(n), :]
        o_ref[rows(n), :] = (_rms(o) * (gate * _sigmoid(gate))).astype(BF16)


def _retention(rq, rk, rv, rg, decay_f, decay_b):
    b, s, w = rq.shape
    heads = w // RET_DIM
    n_chunks = s // CHUNK
    blk = pl.BlockSpec((None, s, RET_DIM), lambda bi, hi: (bi, 0, hi))
    dec = pl.BlockSpec((None, 1, LANES), lambda bi, hi: (hi, 0, 0))
    df = jnp.broadcast_to(decay_f.reshape(heads, 1, 1), (heads, 1, LANES))
    db = jnp.broadcast_to(decay_b.reshape(heads, 1, 1), (heads, 1, LANES))
    return pl.pallas_call(
        _ret_kernel,
        grid=(b, heads),
        in_specs=[dec, dec, blk, blk, blk, blk],
        out_specs=blk,
        out_shape=jax.ShapeDtypeStruct((b, s, w), BF16),
        scratch_shapes=[pltpu.VMEM((n_chunks, 2 * CHUNK, RET_DIM), BF16)],
        compiler_params=_params("parallel", "parallel"),
        name="retention",
    )(df, db, rq, rk, rv, rg)


def _att_kernel(q_ref, k_ref, vt_ref, o_ref, s0_ref, s1_ref, m0_ref, m1_ref):
    i = pl.program_id(0)
    d = k_ref.shape[1]
    n_units, _, uq = s0_ref.shape
    units = [(r, h) for r in range(n_units // ATT_GROUP) for h in range(ATT_GROUP)]

    @pl.when(i == 0)
    def _():
        s1_ref[...] = jnp.zeros(s1_ref.shape, F32)
        m1_ref[...] = jnp.zeros(m1_ref.shape, F32)

    def step(sw_ref, mw_ref, sr_ref, mr_ref):
        n_keys = k_ref.shape[0]
        kc = min(ATT_KEY_CHUNK, n_keys)
        outs = []
        for u0 in range(0, n_units, 2):
            pair = (u0, u0 + 1)
            qs = {u: q_ref[units[u][0] * uq:(units[u][0] + 1) * uq,
                           units[u][1] * d:(units[u][1] + 1) * d] for u in pair}
            mx = {u: None for u in pair}
            den = {u: None for u in pair}
            acc = {u: None for u in pair}
            for c0 in range(0, n_keys, kc):
                rows = slice(c0, c0 + kc)
                for u in pair:
                    st = lax.dot_general(k_ref[rows, :], qs[u], (((1,), (1,)), ((), ())),
                                         preferred_element_type=F32)
                    sw_ref[u, rows, :] = st
                    cm = jnp.max(st, axis=0, keepdims=True)
                    mx[u] = cm if mx[u] is None else jnp.maximum(mx[u], cm)
                for u in pair:
                    p = jnp.exp2(sr_ref[u, rows, :] - mr_ref[u])
                    ps = jnp.sum(p, axis=0, keepdims=True)
                    pv = _dot(vt_ref[:, rows], p.astype(BF16))
                    den[u] = ps if den[u] is None else den[u] + ps
                    acc[u] = pv if acc[u] is None else acc[u] + pv
            for u in pair:
                mw_ref[u] = mx[u]
                outs.append(acc[u] * (1.0 / den[u]))
            if len(outs) == ATT_GROUP:
                r = units[u0][0]
                o_ref[r * uq:(r + 1) * uq, :] = jnp.concatenate(outs, axis=0).T.astype(BF16)
                outs = []

    @pl.when(i % 2 == 0)
    def _():
        step(s0_ref, m0_ref, s1_ref, m1_ref)

    @pl.when(i % 2 == 1)
    def _():
        step(s1_ref, m1_ref, s0_ref, m0_ref)


def _attention(aq, ak, avt):
    b, s, w = aq.shape
    tq = min(ATT_TQ, s)
    uq = min(ATT_UNIT_Q, tq)
    gw = ATT_GROUP * ATT_DIM
    nq = s // tq
    n_tiles = b * ATT_KV_HEADS * nq
    n_units = ATT_GROUP * (tq // uq)

    def tile(t):
        return t // (ATT_KV_HEADS * nq), (t // nq) % ATT_KV_HEADS, t % nq

    def score_tile(i):
        return tile(jnp.minimum(i, n_tiles - 1))

    def softmax_tile(i):
        return tile(jnp.maximum(i - 1, 0))

    def qmap(bgq):
        bi, gi, qi = bgq
        return bi, qi, gi

    def kvmap(bgq):
        bi, gi, _ = bgq
        return bi, gi, 0, 0

    return pl.pallas_call(
        _att_kernel,
        grid=(n_tiles + 1,),
        in_specs=[pl.BlockSpec((None, tq, gw), lambda i: qmap(score_tile(i))),
                  pl.BlockSpec((None, None, s, ATT_DIM), lambda i: kvmap(score_tile(i))),
                  pl.BlockSpec((None, None, ATT_DIM, s), lambda i: kvmap(softmax_tile(i)))],
        out_specs=pl.BlockSpec((None, tq, gw), lambda i: qmap(softmax_tile(i))),
        out_shape=jax.ShapeDtypeStruct((b, s, w), BF16),
        scratch_shapes=[pltpu.VMEM((n_units, s, uq), F32), pltpu.VMEM((n_units, s, uq), F32),
                        pltpu.VMEM((n_units, 1, uq), F32), pltpu.VMEM((n_units, 1, uq), F32)],
        compiler_params=_params("arbitrary"),
        name="attention",
    )(aq, ak, avt)


def _gelu(x):
    return 0.5 * x * (1.0 + lax.erf(x * (2.0 ** -0.5)))


def _odd_kernel(x_ref, g_ref, wi_ref, lg_ref, lb_ref, ws_ref, bs_ref, wo_ref, o_ref):
    x = x_ref[...]
    tm = x.shape[0]
    half = wo_ref.shape[0]
    gw = half // GMLP_GROUPS
    hb = (_rms(x) * g_ref[...]).astype(BF16)
    v = _gelu(_dot(hb, wi_ref[:, half:]))
    mu = jnp.mean(v, axis=-1, keepdims=True)
    vc = v - mu
    var = jnp.mean(vc * vc, axis=-1, keepdims=True)
    vn = (vc * lax.rsqrt(var + EPS) * lg_ref[...] + lb_ref[...]).astype(BF16)
    gated = []
    pair = 2 * gw
    for c0 in range(0, half, pair):
        u = _gelu(_dot(hb, wi_ref[:, c0:c0 + pair]))
        mixed = jnp.concatenate(
            [jnp.concatenate(
                [_dot(ws_ref[g], vn[n * CHUNK:(n + 1) * CHUNK, g * gw:(g + 1) * gw]) + bs_ref[g]
                 for n in range(tm // CHUNK)], axis=0)
             for g in (c0 // gw, c0 // gw + 1)], axis=1)
        gated.append((u * mixed).astype(BF16))
    o_ref[...] = x + _dot(jnp.concatenate(gated, axis=1), wo_ref[...])


def _odd(x2, gain, w_in, ln_g, ln_b, w_s, b_s, w_out):
    t, d = x2.shape
    tm = min(ODD_TM, t)
    half = w_out.shape[0]
    gw = half // GMLP_GROUPS
    assert w_in.shape == (d, 2 * half) and w_s.shape == (GMLP_GROUPS, CHUNK, CHUNK)
    bs_full = jnp.broadcast_to(b_s[:, :, None], (GMLP_GROUPS, CHUNK, gw))
    row = pl.BlockSpec((tm, d), lambda i: (i, 0))
    return pl.pallas_call(
        _odd_kernel,
        grid=(t // tm,),
        in_specs=[row, _resident((1, d)), _resident((d, 2 * half)), _resident((1, half)),
                  _resident((1, half)), _resident((GMLP_GROUPS, CHUNK, CHUNK)),
                  _resident((GMLP_GROUPS, CHUNK, gw)), _resident((half, d))],
        out_specs=row,
        out_shape=jax.ShapeDtypeStruct((t, d), F32),
        compiler_params=_params("parallel"),
        name="odd_mixer",
    )(x2, gain.reshape(1, d), w_in.astype(BF16), ln_g.reshape(1, half), ln_b.reshape(1, half),
      w_s.astype(BF16), bs_full, w_out.astype(BF16))


def kernel(x, ln_ffn1, ffn1_w_gate, ffn1_w_up, ffn1_w_down, ln_mix, even_w_in, ret_decay_fwd, ret_decay_bwd, att_q_norm, att_k_norm, even_w_out, odd_w_in, sgu_ln_g, sgu_ln_b, sgu_w_s, sgu_b_s, odd_w_out, ln_ffn2, ffn2_w_gate, ffn2_w_up, ffn2_w_down, final_norm):
    b, s, d = x.shape
    depth = ln_ffn1.shape[0]
    t = b * s
    x2 = x.reshape(t, d)
    for l in range(depth):
        x2 = _ffn(x2, ln_ffn1[l], ffn1_w_gate[l], ffn1_w_up[l], ffn1_w_down[l])
        mixer_out = None
        if l % 2 == 0:
            e = l // 2
            rq, rk, rv, rg, aq, ak, avt = _even_in(x2.reshape(b, s, d), ln_mix[l], even_w_in[e],
                                                   att_q_norm[e], att_k_norm[e])
            ret = _retention(rq, rk, rv, rg, ret_decay_fwd[e], ret_decay_bwd[e])
            att = _attention(aq, ak, avt)
            mixer_out = (ret.reshape(t, -1), att.reshape(t, -1), even_w_out[e])
        else:
            o = l // 2
            x2 = _odd(x2, ln_mix[l], odd_w_in[o], sgu_ln_g[o], sgu_ln_b[o], sgu_w_s[o],
                      sgu_b_s[o], odd_w_out[o])
        x2 = _ffn(x2, ln_ffn2[l], ffn2_w_gate[l], ffn2_w_up[l], ffn2_w_down[l],
                  mixer_out=mixer_out, final_gain=final_norm if l == depth - 1 else None)
    return x2.reshape(b, s, d)
```

```python
import functools

import numpy as np
import jax
import jax.numpy as jnp
from jax import lax
from jax.experimental import pallas as pl
from jax.experimental.pallas import tpu as pltpu

F32 = jnp.float32
BF16 = jnp.bfloat16

EPS = 1e-6
GRID_W = 64
CHUNK = 128
ROPE_THETA = 10000.0
RET_HEADS = 4
RET_DIM = 128
RET_HEADS_PER_STEP = 4
ATT_Q_HEADS = 8
ATT_KV_HEADS = 2
ATT_DIM = 64
ATT_GROUP = ATT_Q_HEADS // ATT_KV_HEADS
ATT_Q_SCALE = ATT_DIM ** -0.5 * float(np.log2(np.e))
GMLP_GROUPS = 8

LANES = 128
VMEM_LIMIT_BYTES = 56 * 1024 * 1024

FFN_TM = 1024
FFN_SUB = 256
EVEN_TM = 512
EVEN_SUB = 256
ODD_TM = 512
ODD_SUB = 256
ATT_TQ = 512
ATT_UNIT_Q = 256
ATT_KEY_CHUNK = 512
MXU_N = 256


def _params(*sem):
    return pltpu.CompilerParams(dimension_semantics=sem, vmem_limit_bytes=VMEM_LIMIT_BYTES)


def _resident(shape):
    nd = len(shape)
    return pl.BlockSpec(shape, lambda *_: (0,) * nd, pipeline_mode=pl.Buffered(1))


def _rms(x):
    return x * lax.rsqrt(jnp.mean(x * x, axis=-1, keepdims=True) + EPS)


def _sigmoid(x):
    return 1.0 / (1.0 + jnp.exp(-x))


def _dot(a, b):
    return jnp.dot(a, b, preferred_element_type=F32)


def _col_chunks(n, step):
    return tuple((c, min(c + step, n)) for c in range(0, n, step))


def _ffn_kernel(*refs, ff_chunks, mixer_out, final):
    x_ref, g_ref, wg_ref, wu_ref, wd_ref = refs[:5]
    extra = list(refs[5:-1])
    o_ref = refs[-1]
    if mixer_out:
        r_ref, a_ref, wm_ref = extra[:3]
        extra = extra[3:]
        rw = r_ref.shape[1]
    if final:
        fg_ref, = extra
    tm = x_ref.shape[0]
    sub = min(2 * FFN_SUB if mixer_out else FFN_SUB, tm)
    for r0 in range(0, tm, sub):
        rows = slice(r0, r0 + sub)
        x = x_ref[rows, :]
        if mixer_out:
            x = x + _dot(r_ref[rows, :], wm_ref[:rw, :]) + _dot(a_ref[rows, :], wm_ref[rw:, :])
        hb = (_rms(x) * g_ref[...]).astype(BF16)
        acc = None
        for c0, c1 in ff_chunks:
            gate = _dot(hb, wg_ref[:, c0:c1])
            up = _dot(hb, wu_ref[:, c0:c1])
            a = (gate * _sigmoid(gate) * up).astype(BF16)
            d = _dot(a, wd_ref[c0:c1, :])
            acc = d if acc is None else acc + d
        y = x + 0.5 * acc
        if final:
            y = _rms(y) * fg_ref[...]
        o_ref[rows, :] = y


def _ffn(x2, gain, wg, wu, wd, mixer_out=None, final_gain=None):
    t, d = x2.shape
    ff = wg.shape[1]
    tm = min(FFN_TM, t)

    def row(w):
        return pl.BlockSpec((tm, w), lambda i: (i, 0))

    in_specs = [row(d), _resident((1, d)), _resident((d, ff)), _resident((d, ff)),
                _resident((ff, d))]
    args = [x2, gain.reshape(1, d), wg.astype(BF16), wu.astype(BF16), wd.astype(BF16)]
    if mixer_out is not None:
        ret2, att2, w_mix = mixer_out
        in_specs += [row(ret2.shape[1]), row(att2.shape[1]), _resident(w_mix.shape)]
        args += [ret2, att2, w_mix.astype(BF16)]
    if final_gain is not None:
        in_specs.append(_resident((1, d)))
        args.append(final_gain.reshape(1, d))
    name = "ffn" + ("_mix" if mixer_out is not None else "") + ("_final" if final_gain is not None else "")
    return pl.pallas_call(
        functools.partial(_ffn_kernel, ff_chunks=_col_chunks(ff, 4 * MXU_N),
                          mixer_out=mixer_out is not None, final=final_gain is not None),
        grid=(t // tm,),
        in_specs=in_specs,
        out_specs=row(d),
        out_shape=jax.ShapeDtypeStruct((t, d), F32),
        compiler_params=_params("parallel"),
        name=name,
    )(*args)


def _rope(xb, cos, sin_signed, even_lane):
    nxt = pltpu.roll(xb, LANES - 1, 1)
    prv = pltpu.roll(xb, 1, 1)
    return xb * cos + jnp.where(even_lane, nxt, prv) * sin_signed


def _group_mean_sq(x, e_ref):
    sq = x * x
    hi = sq.astype(BF16)
    lo = (sq - hi.astype(F32)).astype(BF16)
    n = x.shape[1]
    e = e_ref[:n, :n]
    return _dot(hi, e) + _dot(lo, e)


def _even_in_kernel(x_ref, g_ref, w_ref, e_ref, cr_ref, sr_ref, ca_ref, sa_ref, gq_ref, gk_ref,
                    rq_ref, rk_ref, rv_ref, rg_ref, aq_ref, ak_ref, avt_ref, *, splits):
    o_rq, o_rk, o_rv, o_rg, o_aq, o_ak, o_av, o_end = splits
    tm = x_ref.shape[0]
    sub = min(EVEN_SUB, tm)
    even_lane = (lax.broadcasted_iota(jnp.int32, (sub, LANES), 1) % 2) == 0
    for r0 in range(0, tm, sub):
        rows = slice(r0, r0 + sub)
        hb = (_rms(x_ref[rows, :]) * g_ref[...]).astype(BF16)
        cr, sr = cr_ref[rows, :], sr_ref[rows, :]
        ca, sa = ca_ref[rows, :], sa_ref[rows, :]

        def proj(c0, c1):
            return _dot(hb, w_ref[:, c0:c1])

        aq = proj(o_aq, o_ak)
        aq = aq * lax.rsqrt(_group_mean_sq(aq, e_ref) + EPS) * gq_ref[...]
        for j in range(aq.shape[1] // LANES):
            sl = slice(j * LANES, (j + 1) * LANES)
            aq_ref[rows, sl] = (_rope(aq[:, sl], ca, sa, even_lane) * ATT_Q_SCALE).astype(BF16)
        ak = proj(o_ak, o_av)
        ak = ak * lax.rsqrt(_group_mean_sq(ak, e_ref) + EPS) * gk_ref[...]
        ak = _rope(ak, ca, sa, even_lane).astype(BF16)
        avt = proj(o_av, o_end).T.astype(BF16)
        for h in range(ATT_KV_HEADS):
            sl = slice(h * ATT_DIM, (h + 1) * ATT_DIM)
            ak_ref[h, rows, :] = ak[:, sl]
            avt_ref[h, :, rows] = avt[sl, :]

        rq = proj(o_rq, o_rk)
        rk = proj(o_rk, o_rv)
        for h in range(RET_HEADS):
            sl = slice(h * RET_DIM, (h + 1) * RET_DIM)
            rq_ref[rows, sl] = _rope(rq[:, sl], cr, sr, even_lane).astype(BF16)
            rk_ref[rows, sl] = (_rope(rk[:, sl], cr, sr, even_lane) * RET_DIM ** -0.5).astype(BF16)
        rv_ref[rows, :] = proj(o_rv, o_rg).astype(BF16)
        rg_ref[rows, :] = proj(o_rg, o_aq)


def _rope_tables(seq, head_dim):
    rows = seq // GRID_W
    row = jnp.repeat(jnp.arange(rows), GRID_W).astype(F32)
    col = jnp.tile(jnp.arange(GRID_W), rows).astype(F32)
    axis_dim = head_dim // 2
    freqs = ROPE_THETA ** (-jnp.arange(0, axis_dim, 2, dtype=F32) / axis_dim)
    ang = jnp.concatenate([row[:, None] * freqs[None, :], col[:, None] * freqs[None, :]], axis=-1)
    cos = jnp.repeat(jnp.cos(ang), 2, axis=-1)
    sign = jnp.tile(jnp.array([-1.0, 1.0], F32), head_dim // 2)
    sin = jnp.repeat(jnp.sin(ang), 2, axis=-1) * sign[None, :]
    reps = LANES // head_dim
    return jnp.tile(cos, (1, reps)), jnp.tile(sin, (1, reps))


def _even_in(x3, gain, w_in, gq, gk):
    b, s, d = x3.shape
    tm = min(EVEN_TM, s)
    ret_w = RET_HEADS * RET_DIM
    att_w = ATT_Q_HEADS * ATT_DIM
    kv_w = ATT_KV_HEADS * ATT_DIM
    sizes = (ret_w, ret_w, ret_w, ret_w, att_w, kv_w, kv_w)
    splits = (0,) + tuple(int(v) for v in np.cumsum(sizes))
    n_in = splits[-1]
    assert w_in.shape == (d, n_in)
    cr, sr = _rope_tables(s, RET_DIM)
    ca, sa = _rope_tables(s, ATT_DIM)
    grp = np.arange(att_w) // ATT_DIM
    e = jnp.asarray((grp[:, None] == grp[None, :]).astype(np.float32) / ATT_DIM, BF16)
    gq_t = jnp.tile(gq, ATT_Q_HEADS).reshape(1, att_w)
    gk_t = jnp.tile(gk, ATT_KV_HEADS).reshape(1, kv_w)

    def row(w):
        return pl.BlockSpec((None, tm, w), lambda bi, si: (bi, si, 0))

    tab = pl.BlockSpec((tm, LANES), lambda bi, si: (si, 0))
    k_out = pl.BlockSpec((None, ATT_KV_HEADS, tm, ATT_DIM), lambda bi, si: (bi, 0, si, 0))
    vt_out = pl.BlockSpec((None, ATT_KV_HEADS, ATT_DIM, tm), lambda bi, si: (bi, 0, 0, si))
    return pl.pallas_call(
        functools.partial(_even_in_kernel, splits=splits),
        grid=(b, s // tm),
        in_specs=[row(d), _resident((1, d)), _resident((d, n_in)), _resident((att_w, att_w)),
                  tab, tab, tab, tab, _resident((1, att_w)), _resident((1, kv_w))],
        out_specs=[row(ret_w), row(ret_w), row(ret_w), row(ret_w), row(att_w), k_out, vt_out],
        out_shape=[jax.ShapeDtypeStruct((b, s, ret_w), BF16),
                   jax.ShapeDtypeStruct((b, s, ret_w), BF16),
                   jax.ShapeDtypeStruct((b, s, ret_w), BF16),
                   jax.ShapeDtypeStruct((b, s, ret_w), F32),
                   jax.ShapeDtypeStruct((b, s, att_w), BF16),
                   jax.ShapeDtypeStruct((b, ATT_KV_HEADS, s, ATT_DIM), BF16),
                   jax.ShapeDtypeStruct((b, ATT_KV_HEADS, ATT_DIM, s), BF16)],
        compiler_params=_params("parallel", "parallel"),
        name="even_in",
    )(x3, gain.reshape(1, d), w_in.astype(BF16), e, cr, sr, ca, sa, gq_t, gk_t)


def _ret_kernel(df_ref, db_ref, q_ref, k_ref, v_ref, g_ref, o_ref, st_ref):
    c = CHUNK
    d = RET_DIM
    n_chunks = q_ref.shape[0] // c
    heads = range(q_ref.shape[1] // d)
    ri = lax.broadcasted_iota(jnp.int32, (c, c), 0).astype(F32)
    ci = lax.broadcasted_iota(jnp.int32, (c, c), 1).astype(F32)
    diff = ri - ci
    tdot = functools.partial(lax.dot_general, dimension_numbers=(((0,), (0,)), ((), ())),
                             preferred_element_type=F32)
    mask, kw_f, qw_f, kw_b, qw_b, dec_f, dec_b = [], [], [], [], [], [], []
    for h in heads:
        lam_f = jnp.exp(df_ref[h])
        lam_b = jnp.exp(db_ref[h])
        mask.append(jnp.where(diff >= 0,
                              jnp.exp(-lam_f * jnp.maximum(diff, 0.0)),
                              jnp.exp(-lam_b * jnp.maximum(-diff, 0.0))))
        kw_f.append(jnp.exp(-lam_f * (c - 1.0 - ri)))
        qw_f.append(jnp.exp(-lam_f * (ri + 1.0)))
        kw_b.append(jnp.exp(-lam_b * ri))
        qw_b.append(jnp.exp(-lam_b * (c - ri)))
        dec_f.append(jnp.exp(-lam_f * c))
        dec_b.append(jnp.exp(-lam_b * c))

    def blk(ref, n, h):
        return ref[n * c:(n + 1) * c, h * d:(h + 1) * d]

    st = [jnp.zeros((c, c), F32) for _ in heads]
    for n in range(n_chunks):
        for h in heads:
            st_ref[h, n, :c, :] = st[h].astype(BF16)
            k = blk(k_ref, n, h).astype(F32)
            st[h] = dec_f[h] * st[h] + tdot((k * kw_f[h]).astype(BF16), blk(v_ref, n, h))
    st = [jnp.zeros((c, c), F32) for _ in heads]
    for n in reversed(range(n_chunks)):
        for h in heads:
            st_ref[h, n, c:, :] = st[h].astype(BF16)
            k = blk(k_ref, n, h).astype(F32)
            st[h] = dec_b[h] * st[h] + tdot((k * kw_b[h]).astype(BF16), blk(v_ref, n, h))

    for n in range(n_chunks):
        for h in heads:
            qb = blk(q_ref, n, h)
            q = qb.astype(F32)
            s = lax.dot_general(qb, blk(k_ref, n, h), (((1,), (1,)), ((), ())),
                                preferred_element_type=F32) * mask[h]
            qq = jnp.concatenate([(q * qw_f[h]).astype(BF16), (q * qw_b[h]).astype(BF16)], axis=1)
            o = _dot(s.astype(BF16), blk(v_ref, n, h)) + _dot(qq, st_ref[h, n])
            gate = blk(g_ref, n, h)
            o_ref[n * c:(n + 1) * c, h * d:(h + 1) * d] = (
                _rms(o) * (gate * _sigmoid(gate))).astype(BF16)


def _retention(rq, rk, rv, rg, decay_f, decay_b):
    b, s, w = rq.shape
    heads = w // RET_DIM
    hp = RET_HEADS_PER_STEP
    n_chunks = s // CHUNK
    blk = pl.BlockSpec((None, s, hp * RET_DIM), lambda bi, hi: (bi, 0, hi))
    dec = pl.BlockSpec((hp, 1, LANES), lambda bi, hi: (hi, 0, 0))
    df = jnp.broadcast_to(decay_f.reshape(heads, 1, 1), (heads, 1, LANES))
    db = jnp.broadcast_to(decay_b.reshape(heads, 1, 1), (heads, 1, LANES))
    return pl.pallas_call(
        _ret_kernel,
        grid=(b, heads // hp),
        in_specs=[dec, dec, blk, blk, blk, blk],
        out_specs=blk,
        out_shape=jax.ShapeDtypeStruct((b, s, w), BF16),
        scratch_shapes=[pltpu.VMEM((hp, n_chunks, 2 * CHUNK, RET_DIM), BF16)],
        compiler_params=_params("parallel", "parallel"),
        name="retention",
    )(df, db, rq, rk, rv, rg)


def _att_kernel(q_ref, k_ref, vt_ref, o_ref, s0_ref, s1_ref, m0_ref, m1_ref):
    i = pl.program_id(0)
    d = k_ref.shape[1]
    n_units, _, uq = s0_ref.shape
    units = [(r, h) for r in range(n_units // ATT_GROUP) for h in range(ATT_GROUP)]

    @pl.when(i == 0)
    def _():
        s1_ref[...] = jnp.zeros(s1_ref.shape, F32)
        m1_ref[...] = jnp.zeros(m1_ref.shape, F32)

    def step(sw_ref, mw_ref, sr_ref, mr_ref):
        n_keys = k_ref.shape[0]
        kc = min(ATT_KEY_CHUNK, n_keys)
        outs = []
        for u0 in range(0, n_units, 2):
            pair = (u0, u0 + 1)
            qs = {u: q_ref[units[u][0] * uq:(units[u][0] + 1) * uq,
                           units[u][1] * d:(units[u][1] + 1) * d] for u in pair}
            mx = {u: None for u in pair}
            den = {u: None for u in pair}
            acc = {u: None for u in pair}
            for c0 in range(0, n_keys, kc):
                rows = slice(c0, c0 + kc)
                for u in pair:
                    st = lax.dot_general(k_ref[rows, :], qs[u], (((1,), (1,)), ((), ())),
                                         preferred_element_type=F32)
                    sw_ref[u, rows, :] = st
                    cm = jnp.max(st, axis=0, keepdims=True)
                    mx[u] = cm if mx[u] is None else jnp.maximum(mx[u], cm)
                for u in pair:
                    p = jnp.exp2(sr_ref[u, rows, :] - mr_ref[u])
                    ps = jnp.sum(p, axis=0, keepdims=True)
                    pv = _dot(vt_ref[:, rows], p.astype(BF16))
                    den[u] = ps if den[u] is None else den[u] + ps
                    acc[u] = pv if acc[u] is None else acc[u] + pv
            for u in pair:
                mw_ref[u] = mx[u]
                outs.append(acc[u] * (1.0 / den[u]))
            if len(outs) == ATT_GROUP:
                r = units[u0][0]
                o_ref[r * uq:(r + 1) * uq, :] = jnp.concatenate(outs, axis=0).T.astype(BF16)
                outs = []

    @pl.when(i % 2 == 0)
    def _():
        step(s0_ref, m0_ref, s1_ref, m1_ref)

    @pl.when(i % 2 == 1)
    def _():
        step(s1_ref, m1_ref, s0_ref, m0_ref)


def _attention(aq, ak, avt):
    b, s, w = aq.shape
    tq = min(ATT_TQ, s)
    uq = min(ATT_UNIT_Q, tq)
    gw = ATT_GROUP * ATT_DIM
    nq = s // tq
    n_tiles = b * ATT_KV_HEADS * nq
    n_units = ATT_GROUP * (tq // uq)

    def tile(t):
        return t // (ATT_KV_HEADS * nq), (t // nq) % ATT_KV_HEADS, t % nq

    def score_tile(i):
        return tile(jnp.minimum(i, n_tiles - 1))

    def softmax_tile(i):
        return tile(jnp.maximum(i - 1, 0))

    def qmap(bgq):
        bi, gi, qi = bgq
        return bi, qi, gi

    def kvmap(bgq):
        bi, gi, _ = bgq
        return bi, gi, 0, 0

    return pl.pallas_call(
        _att_kernel,
        grid=(n_tiles + 1,),
        in_specs=[pl.BlockSpec((None, tq, gw), lambda i: qmap(score_tile(i))),
                  pl.BlockSpec((None, None, s, ATT_DIM), lambda i: kvmap(score_tile(i))),
                  pl.BlockSpec((None, None, ATT_DIM, s), lambda i: kvmap(softmax_tile(i)))],
        out_specs=pl.BlockSpec((None, tq, gw), lambda i: qmap(softmax_tile(i))),
        out_shape=jax.ShapeDtypeStruct((b, s, w), BF16),
        scratch_shapes=[pltpu.VMEM((n_units, s, uq), F32), pltpu.VMEM((n_units, s, uq), F32),
                        pltpu.VMEM((n_units, 1, uq), F32), pltpu.VMEM((n_units, 1, uq), F32)],
        compiler_params=_params("arbitrary"),
        name="attention",
    )(aq, ak, avt)


def _gelu2(x):
    return x + x * lax.erf(x * (2.0 ** -0.5))


def _odd_kernel(x_ref, g_ref, wi_ref, lg_ref, lb_ref, ws_ref, bs_ref, wo_ref, o_ref):
    half = wo_ref.shape[0]
    gw = half // GMLP_GROUPS
    pair = 2 * gw
    tm = x_ref.shape[0]
    sub = min(ODD_SUB, tm)
    blocks = [slice(r0, r0 + sub) for r0 in range(0, tm, sub)]

    hbs, vns = [], []
    for rows in blocks:
        hb = (_rms(x_ref[rows, :]) * g_ref[...]).astype(BF16)
        v2 = _gelu2(_dot(hb, wi_ref[:, half:]))
        mu = jnp.mean(v2, axis=-1, keepdims=True)
        vc = v2 - mu
        var = jnp.mean(vc * vc, axis=-1, keepdims=True)
        hbs.append(hb)
        vns.append((vc * lax.rsqrt(var + 4.0 * EPS) * lg_ref[...] + lb_ref[...]).astype(BF16))

    for rows, hb, vn in zip(blocks, hbs, vns):
        gated = []
        for c0 in range(0, half, pair):
            u = _gelu2(_dot(hb, wi_ref[:, c0:c0 + pair]))
            mixed = jnp.concatenate(
                [jnp.concatenate(
                    [_dot(ws_ref[g], vn[n * CHUNK:(n + 1) * CHUNK, g * gw:(g + 1) * gw]) + bs_ref[g]
                     for n in range(sub // CHUNK)], axis=0)
                 for g in (c0 // gw, c0 // gw + 1)], axis=1)
            gated.append((u * mixed).astype(BF16))
        o_ref[rows, :] = x_ref[rows, :] + _dot(jnp.concatenate(gated, axis=1), wo_ref[...])


def _odd(x2, gain, w_in, ln_g, ln_b, w_s, b_s, w_out):
    t, d = x2.shape
    tm = min(ODD_TM, t)
    half = w_out.shape[0]
    gw = half // GMLP_GROUPS
    assert w_in.shape == (d, 2 * half) and w_s.shape == (GMLP_GROUPS, CHUNK, CHUNK)
    bs_full = jnp.broadcast_to(b_s[:, :, None], (GMLP_GROUPS, CHUNK, gw))
    row = pl.BlockSpec((tm, d), lambda i: (i, 0))
    return pl.pallas_call(
        _odd_kernel,
        grid=(t // tm,),
        in_specs=[row, _resident((1, d)), _resident((d, 2 * half)), _resident((1, half)),
                  _resident((1, half)), _resident((GMLP_GROUPS, CHUNK, CHUNK)),
                  _resident((GMLP_GROUPS, CHUNK, gw)), _resident((half, d))],
        out_specs=row,
        out_shape=jax.ShapeDtypeStruct((t, d), F32),
        compiler_params=_params("parallel"),
        name="odd_mixer",
    )(x2, gain.reshape(1, d), w_in.astype(BF16), ln_g.reshape(1, half), ln_b.reshape(1, half),
      w_s.astype(BF16), bs_full, (0.5 * w_out).astype(BF16))


def kernel(x, ln_ffn1, ffn1_w_gate, ffn1_w_up, ffn1_w_down, ln_mix, even_w_in, ret_decay_fwd, ret_decay_bwd, att_q_norm, att_k_norm, even_w_out, odd_w_in, sgu_ln_g, sgu_ln_b, sgu_w_s, sgu_b_s, odd_w_out, ln_ffn2, ffn2_w_gate, ffn2_w_up, ffn2_w_down, final_norm):
    b, s, d = x.shape
    depth = ln_ffn1.shape[0]
    t = b * s
    x2 = x.reshape(t, d)
    for l in range(depth):
        x2 = _ffn(x2, ln_ffn1[l], ffn1_w_gate[l], ffn1_w_up[l], ffn1_w_down[l])
        mixer_out = None
        if l % 2 == 0:
            e = l // 2
            rq, rk, rv, rg, aq, ak, avt = _even_in(x2.reshape(b, s, d), ln_mix[l], even_w_in[e],
                                                   att_q_norm[e], att_k_norm[e])
            ret = _retention(rq, rk, rv, rg, ret_decay_fwd[e], ret_decay_bwd[e])
            att = _attention(aq, ak, avt)
            mixer_out = (ret.reshape(t, -1), att.reshape(t, -1), even_w_out[e])
        else:
            o = l // 2
            x2 = _odd(x2, ln_mix[l], odd_w_in[o], sgu_ln_g[o], sgu_ln_b[o], sgu_w_s[o],
                      sgu_b_s[o], odd_w_out[o])
        x2 = _ffn(x2, ln_ffn2[l], ffn2_w_gate[l], ffn2_w_up[l], ffn2_w_down[l],
                  mixer_out=mixer_out, final_gain=final_norm if l == depth - 1 else None)
    return x2.reshape(b, s, d)
```

```python
import functools

import numpy as np
import jax
import jax.numpy as jnp
from jax import lax
from jax.experimental import pallas as pl
from jax.experimental.pallas import tpu as pltpu

F32 = jnp.float32
BF16 = jnp.bfloat16

EPS = 1e-6
GRID_W = 64
CHUNK = 128
ROPE_THETA = 10000.0
RET_HEADS = 4
RET_DIM = 128
RET_HEADS_PER_STEP = 4
ATT_Q_HEADS = 8
ATT_KV_HEADS = 2
ATT_DIM = 64
ATT_GROUP = ATT_Q_HEADS // ATT_KV_HEADS
ATT_Q_SCALE = ATT_DIM ** -0.5 * float(np.log2(np.e))
GMLP_GROUPS = 8

LANES = 128
VMEM_LIMIT_BYTES = 56 * 1024 * 1024

FFN_TM = 1024
STAGE_ROWS = 256
FFN_SUB = 256
EVEN_TM = 512
EVEN_SUB = 256
ODD_TM = 512
ODD_SUB = 256
ATT_TQ = 512
ATT_UNIT_Q = 256
ATT_KEY_CHUNK = 512
MXU_N = 256


def _params(*sem):
    return pltpu.CompilerParams(dimension_semantics=sem, vmem_limit_bytes=VMEM_LIMIT_BYTES)


def _resident(shape):
    nd = len(shape)
    return pl.BlockSpec(shape, lambda *_: (0,) * nd, pipeline_mode=pl.Buffered(1))


def _rms(x):
    return x * lax.rsqrt(jnp.mean(x * x, axis=-1, keepdims=True) + EPS)


def _sigmoid(x):
    return 1.0 / (1.0 + jnp.exp(-x))


def _dot(a, b):
    return jnp.dot(a, b, preferred_element_type=F32)


def _col_chunks(n, step):
    return tuple((c, min(c + step, n)) for c in range(0, n, step))


def _stage_weights(pairs, stage_ref, sem):
    jobs = [(src, dst, r0) for src, dst in pairs for r0 in range(0, dst.shape[0], STAGE_ROWS)]

    def copy(j):
        src, dst, r0 = jobs[j]
        slot = j % 2
        return pltpu.make_async_copy(src.at[pl.ds(r0, STAGE_ROWS), :],
                                     stage_ref.at[slot, :, pl.ds(0, dst.shape[1])], sem.at[slot])

    copy(0).start()
    for j, (src, dst, r0) in enumerate(jobs):
        if j + 1 < len(jobs):
            copy(j + 1).start()
        copy(j).wait()
        dst[pl.ds(r0, STAGE_ROWS), :] = stage_ref[j % 2, :, :dst.shape[1]].astype(BF16)


def _ffn_kernel(*refs, layer, ff_chunks, mixer_out, final):
    wg_ref, wu_ref, wd_ref, stage_ref, sem = refs[-5:]
    x_ref, g_ref, wg_hbm, wu_hbm, wd_hbm = refs[:5]
    extra = list(refs[5:-6])
    o_ref = refs[-6]
    if mixer_out:
        r_ref, a_ref, wm_ref = extra[:3]
        extra = extra[3:]
        rw = r_ref.shape[1]
    if final:
        fg_ref, = extra

    @pl.when(pl.program_id(0) == 0)
    def _():
        _stage_weights([(wg_hbm.at[layer], wg_ref), (wu_hbm.at[layer], wu_ref),
                        (wd_hbm.at[layer], wd_ref)], stage_ref, sem)

    tm = x_ref.shape[0]
    sub = min(2 * FFN_SUB if mixer_out else FFN_SUB, tm)
    for r0 in range(0, tm, sub):
        rows = slice(r0, r0 + sub)
        x = x_ref[rows, :]
        if mixer_out:
            x = x + _dot(r_ref[rows, :], wm_ref[:rw, :]) + _dot(a_ref[rows, :], wm_ref[rw:, :])
        hb = (_rms(x) * g_ref[...]).astype(BF16)
        acc = None
        for c0, c1 in ff_chunks:
            gate = _dot(hb, wg_ref[:, c0:c1])
            up = _dot(hb, wu_ref[:, c0:c1])
            a = (gate * _sigmoid(gate) * up).astype(BF16)
            d = _dot(a, wd_ref[c0:c1, :])
            acc = d if acc is None else acc + d
        y = x + 0.5 * acc
        if final:
            y = _rms(y) * fg_ref[...]
        o_ref[rows, :] = y


def _ffn(x2, gain, wg_all, wu_all, wd_all, layer, mixer_out=None, final_gain=None):
    t, d = x2.shape
    ff = wg_all.shape[2]
    tm = min(FFN_TM, t)
    assert d % STAGE_ROWS == 0 and ff % STAGE_ROWS == 0

    def row(w):
        return pl.BlockSpec((tm, w), lambda i: (i, 0))

    hbm = pl.BlockSpec(memory_space=pl.ANY)
    in_specs = [row(d), _resident((1, d)), hbm, hbm, hbm]
    args = [x2, gain.reshape(1, d), wg_all, wu_all, wd_all]
    if mixer_out is not None:
        ret2, att2, w_mix = mixer_out
        in_specs += [row(ret2.shape[1]), row(att2.shape[1]), _resident(w_mix.shape)]
        args += [ret2, att2, w_mix.astype(BF16)]
    if final_gain is not None:
        in_specs.append(_resident((1, d)))
        args.append(final_gain.reshape(1, d))
    name = "ffn" + ("_mix" if mixer_out is not None else "") + ("_final" if final_gain is not None else "")
    return pl.pallas_call(
        functools.partial(_ffn_kernel, layer=layer, ff_chunks=_col_chunks(ff, 4 * MXU_N),
                          mixer_out=mixer_out is not None, final=final_gain is not None),
        grid=(t // tm,),
        in_specs=in_specs,
        out_specs=row(d),
        out_shape=jax.ShapeDtypeStruct((t, d), F32),
        scratch_shapes=[pltpu.VMEM((d, ff), BF16), pltpu.VMEM((d, ff), BF16), pltpu.VMEM((ff, d), BF16),
                        pltpu.VMEM((2, STAGE_ROWS, max(d, ff)), F32), pltpu.SemaphoreType.DMA((2,))],
        compiler_params=_params("arbitrary"),
        name=name,
    )(*args)


def _rope(xb, cos, sin_signed, even_lane):
    nxt = pltpu.roll(xb, LANES - 1, 1)
    prv = pltpu.roll(xb, 1, 1)
    return xb * cos + jnp.where(even_lane, nxt, prv) * sin_signed


def _group_mean_sq(x, e_ref):
    sq = x * x
    hi = sq.astype(BF16)
    lo = (sq - hi.astype(F32)).astype(BF16)
    n = x.shape[1]
    e = e_ref[:n, :n]
    return _dot(hi, e) + _dot(lo, e)


def _even_in_kernel(x_ref, g_ref, w_ref, e_ref, cr_ref, sr_ref, ca_ref, sa_ref, gq_ref, gk_ref,
                    rq_ref, rk_ref, rv_ref, rg_ref, aq_ref, ak_ref, avt_ref, *, splits):
    o_rq, o_rk, o_rv, o_rg, o_aq, o_ak, o_av, o_end = splits
    tm = x_ref.shape[0]
    sub = min(EVEN_SUB, tm)
    even_lane = (lax.broadcasted_iota(jnp.int32, (sub, LANES), 1) % 2) == 0
    for r0 in range(0, tm, sub):
        rows = slice(r0, r0 + sub)
        hb = (_rms(x_ref[rows, :]) * g_ref[...]).astype(BF16)
        cr, sr = cr_ref[rows, :], sr_ref[rows, :]
        ca, sa = ca_ref[rows, :], sa_ref[rows, :]

        def proj(c0, c1):
            return _dot(hb, w_ref[:, c0:c1])

        aq = proj(o_aq, o_ak)
        aq = aq * lax.rsqrt(_group_mean_sq(aq, e_ref) + EPS) * gq_ref[...]
        for j in range(aq.shape[1] // LANES):
            sl = slice(j * LANES, (j + 1) * LANES)
            aq_ref[rows, sl] = (_rope(aq[:, sl], ca, sa, even_lane) * ATT_Q_SCALE).astype(BF16)
        ak = proj(o_ak, o_av)
        ak = ak * lax.rsqrt(_group_mean_sq(ak, e_ref) + EPS) * gk_ref[...]
        ak = _rope(ak, ca, sa, even_lane).astype(BF16)
        avt = proj(o_av, o_end).T.astype(BF16)
        for h in range(ATT_KV_HEADS):
            sl = slice(h * ATT_DIM, (h + 1) * ATT_DIM)
            ak_ref[h, rows, :] = ak[:, sl]
            avt_ref[h, :, rows] = avt[sl, :]

        rq = proj(o_rq, o_rk)
        rk = proj(o_rk, o_rv)
        for h in range(RET_HEADS):
            sl = slice(h * RET_DIM, (h + 1) * RET_DIM)
            rq_ref[rows, sl] = _rope(rq[:, sl], cr, sr, even_lane).astype(BF16)
            rk_ref[rows, sl] = (_rope(rk[:, sl], cr, sr, even_lane) * RET_DIM ** -0.5).astype(BF16)
        rv_ref[rows, :] = proj(o_rv, o_rg).astype(BF16)
        rg_ref[rows, :] = proj(o_rg, o_aq)


def _rope_tables(seq, head_dim):
    rows = seq // GRID_W
    row = jnp.repeat(jnp.arange(rows), GRID_W).astype(F32)
    col = jnp.tile(jnp.arange(GRID_W), rows).astype(F32)
    axis_dim = head_dim // 2
    freqs = ROPE_THETA ** (-jnp.arange(0, axis_dim, 2, dtype=F32) / axis_dim)
    ang = jnp.concatenate([row[:, None] * freqs[None, :], col[:, None] * freqs[None, :]], axis=-1)
    cos = jnp.repeat(jnp.cos(ang), 2, axis=-1)
    sign = jnp.tile(jnp.array([-1.0, 1.0], F32), head_dim // 2)
    sin = jnp.repeat(jnp.sin(ang), 2, axis=-1) * sign[None, :]
    reps = LANES // head_dim
    return jnp.tile(cos, (1, reps)), jnp.tile(sin, (1, reps))


def _even_in(x3, gain, w_in, gq, gk):
    b, s, d = x3.shape
    tm = min(EVEN_TM, s)
    ret_w = RET_HEADS * RET_DIM
    att_w = ATT_Q_HEADS * ATT_DIM
    kv_w = ATT_KV_HEADS * ATT_DIM
    sizes = (ret_w, ret_w, ret_w, ret_w, att_w, kv_w, kv_w)
    splits = (0,) + tuple(int(v) for v in np.cumsum(sizes))
    n_in = splits[-1]
    assert w_in.shape == (d, n_in)
    cr, sr = _rope_tables(s, RET_DIM)
    ca, sa = _rope_tables(s, ATT_DIM)
    grp = np.arange(att_w) // ATT_DIM
    e = jnp.asarray((grp[:, None] == grp[None, :]).astype(np.float32) / ATT_DIM, BF16)
    gq_t = jnp.tile(gq, ATT_Q_HEADS).reshape(1, att_w)
    gk_t = jnp.tile(gk, ATT_KV_HEADS).reshape(1, kv_w)

    def row(w):
        return pl.BlockSpec((None, tm, w), lambda bi, si: (bi, si, 0))

    tab = pl.BlockSpec((tm, LANES), lambda bi, si: (si, 0))
    k_out = pl.BlockSpec((None, ATT_KV_HEADS, tm, ATT_DIM), lambda bi, si: (bi, 0, si, 0))
    vt_out = pl.BlockSpec((None, ATT_KV_HEADS, ATT_DIM, tm), lambda bi, si: (bi, 0, 0, si))
    return pl.pallas_call(
        functools.partial(_even_in_kernel, splits=splits),
        grid=(b, s // tm),
        in_specs=[row(d), _resident((1, d)), _resident((d, n_in)), _resident((att_w, att_w)),
                  tab, tab, tab, tab, _resident((1, att_w)), _resident((1, kv_w))],
        out_specs=[row(ret_w), row(ret_w), row(ret_w), row(ret_w), row(att_w), k_out, vt_out],
        out_shape=[jax.ShapeDtypeStruct((b, s, ret_w), BF16),
                   jax.ShapeDtypeStruct((b, s, ret_w), BF16),
                   jax.ShapeDtypeStruct((b, s, ret_w), BF16),
                   jax.ShapeDtypeStruct((b, s, ret_w), F32),
                   jax.ShapeDtypeStruct((b, s, att_w), BF16),
                   jax.ShapeDtypeStruct((b, ATT_KV_HEADS, s, ATT_DIM), BF16),
                   jax.ShapeDtypeStruct((b, ATT_KV_HEADS, ATT_DIM, s), BF16)],
        compiler_params=_params("parallel", "parallel"),
        name="even_in",
    )(x3, gain.reshape(1, d), w_in.astype(BF16), e, cr, sr, ca, sa, gq_t, gk_t)


def _ret_kernel(df_ref, db_ref, q_ref, k_ref, v_ref, g_ref, o_ref, st_ref):
    c = CHUNK
    d = RET_DIM
    n_chunks = q_ref.shape[0] // c
    heads = range(q_ref.shape[1] // d)
    ri = lax.broadcasted_iota(jnp.int32, (c, c), 0).astype(F32)
    ci = lax.broadcasted_iota(jnp.int32, (c, c), 1).astype(F32)
    diff = ri - ci
    tdot = functools.partial(lax.dot_general, dimension_numbers=(((0,), (0,)), ((), ())),
                             preferred_element_type=F32)
    mask, kw_f, qw_f, kw_b, qw_b, dec_f, dec_b = [], [], [], [], [], [], []
    for h in heads:
        lam_f = jnp.exp(df_ref[h])
        lam_b = jnp.exp(db_ref[h])
        mask.append(jnp.where(diff >= 0,
                              jnp.exp(-lam_f * jnp.maximum(diff, 0.0)),
                              jnp.exp(-lam_b * jnp.maximum(-diff, 0.0))))
        kw_f.append(jnp.exp(-lam_f * (c - 1.0 - ri)))
        qw_f.append(jnp.exp(-lam_f * (ri + 1.0)))
        kw_b.append(jnp.exp(-lam_b * ri))
        qw_b.append(jnp.exp(-lam_b * (c - ri)))
        dec_f.append(jnp.exp(-lam_f * c))
        dec_b.append(jnp.exp(-lam_b * c))

    def blk(ref, n, h):
        return ref[n * c:(n + 1) * c, h * d:(h + 1) * d]

    st = [jnp.zeros((c, c), F32) for _ in heads]
    for n in range(n_chunks):
        for h in heads:
            st_ref[h, n, :c, :] = st[h].astype(BF16)
            k = blk(k_ref, n, h).astype(F32)
            st[h] = dec_f[h] * st[h] + tdot((k * kw_f[h]).astype(BF16), blk(v_ref, n, h))
    st = [jnp.zeros((c, c), F32) for _ in heads]
    for n in reversed(range(n_chunks)):
        for h in heads:
            st_ref[h, n, c:, :] = st[h].astype(BF16)
            k = blk(k_ref, n, h).astype(F32)
            st[h] = dec_b[h] * st[h] + tdot((k * kw_b[h]).astype(BF16), blk(v_ref, n, h))

    for n in range(n_chunks):
        for h in heads:
            qb = blk(q_ref, n, h)
            q = qb.astype(F32)
            s = lax.dot_general(qb, blk(k_ref, n, h), (((1,), (1,)), ((), ())),
                                preferred_element_type=F32) * mask[h]
            qq = jnp.concatenate([(q * qw_f[h]).astype(BF16), (q * qw_b[h]).astype(BF16)], axis=1)
            o = _dot(s.astype(BF16), blk(v_ref, n, h)) + _dot(qq, st_ref[h, n])
            gate = blk(g_ref, n, h)
            o_ref[n * c:(n + 1) * c, h * d:(h + 1) * d] = (
                _rms(o) * (gate * _sigmoid(gate))).astype(BF16)


def _retention(rq, rk, rv, rg, decay_f, decay_b):
    b, s, w = rq.shape
    heads = w // RET_DIM
    hp = RET_HEADS_PER_STEP
    n_chunks = s // CHUNK
    blk = pl.BlockSpec((None, s, hp * RET_DIM), lambda bi, hi: (bi, 0, hi))
    dec = pl.BlockSpec((hp, 1, LANES), lambda bi, hi: (hi, 0, 0))
    df = jnp.broadcast_to(decay_f.reshape(heads, 1, 1), (heads, 1, LANES))
    db = jnp.broadcast_to(decay_b.reshape(heads, 1, 1), (heads, 1, LANES))
    return pl.pallas_call(
        _ret_kernel,
        grid=(b, heads // hp),
        in_specs=[dec, dec, blk, blk, blk, blk],
        out_specs=blk,
        out_shape=jax.ShapeDtypeStruct((b, s, w), BF16),
        scratch_shapes=[pltpu.VMEM((hp, n_chunks, 2 * CHUNK, RET_DIM), BF16)],
        compiler_params=_params("parallel", "parallel"),
        name="retention",
    )(df, db, rq, rk, rv, rg)


def _att_kernel(q_ref, k_ref, vt_ref, o_ref, s0_ref, s1_ref, m0_ref, m1_ref):
    i = pl.program_id(0)
    d = k_ref.shape[1]
    n_units, _, uq = s0_ref.shape
    units = [(r, h) for r in range(n_units // ATT_GROUP) for h in range(ATT_GROUP)]

    @pl.when(i == 0)
    def _():
        s1_ref[...] = jnp.zeros(s1_ref.shape, F32)
        m1_ref[...] = jnp.zeros(m1_ref.shape, F32)

    def step(sw_ref, mw_ref, sr_ref, mr_ref):
        n_keys = k_ref.shape[0]
        kc = min(ATT_KEY_CHUNK, n_keys)
        outs = []
        for u0 in range(0, n_units, 2):
            pair = (u0, u0 + 1)
            qs = {u: q_ref[units[u][0] * uq:(units[u][0] + 1) * uq,
                           units[u][1] * d:(units[u][1] + 1) * d] for u in pair}
            mx = {u: None for u in pair}
            den = {u: None for u in pair}
            acc = {u: None for u in pair}
            for c0 in range(0, n_keys, kc):
                rows = slice(c0, c0 + kc)
                for u in pair:
                    st = lax.dot_general(k_ref[rows, :], qs[u], (((1,), (1,)), ((), ())),
                                         preferred_element_type=F32)
                    sw_ref[u, rows, :] = st
                    cm = jnp.max(st, axis=0, keepdims=True)
                    mx[u] = cm if mx[u] is None else jnp.maximum(mx[u], cm)
                for u in pair:
                    p = jnp.exp2(sr_ref[u, rows, :] - mr_ref[u])
                    ps = jnp.sum(p, axis=0, keepdims=True)
                    pv = _dot(vt_ref[:, rows], p.astype(BF16))
                    den[u] = ps if den[u] is None else den[u] + ps
                    acc[u] = pv if acc[u] is None else acc[u] + pv
            for u in pair:
                mw_ref[u] = mx[u]
                outs.append(acc[u] * (1.0 / den[u]))
            if len(outs) == ATT_GROUP:
                r = units[u0][0]
                o_ref[r * uq:(r + 1) * uq, :] = jnp.concatenate(outs, axis=0).T.astype(BF16)
                outs = []

    @pl.when(i % 2 == 0)
    def _():
        step(s0_ref, m0_ref, s1_ref, m1_ref)

    @pl.when(i % 2 == 1)
    def _():
        step(s1_ref, m1_ref, s0_ref, m0_ref)


def _attention(aq, ak, avt):
    b, s, w = aq.shape
    tq = min(ATT_TQ, s)
    uq = min(ATT_UNIT_Q, tq)
    gw = ATT_GROUP * ATT_DIM
    nq = s // tq
    n_tiles = b * ATT_KV_HEADS * nq
    n_units = ATT_GROUP * (tq // uq)

    def tile(t):
        return t // (ATT_KV_HEADS * nq), (t // nq) % ATT_KV_HEADS, t % nq

    def score_tile(i):
        return tile(jnp.minimum(i, n_tiles - 1))

    def softmax_tile(i):
        return tile(jnp.maximum(i - 1, 0))

    def qmap(bgq):
        bi, gi, qi = bgq
        return bi, qi, gi

    def kvmap(bgq):
        bi, gi, _ = bgq
        return bi, gi, 0, 0

    return pl.pallas_call(
        _att_kernel,
        grid=(n_tiles + 1,),
        in_specs=[pl.BlockSpec((None, tq, gw), lambda i: qmap(score_tile(i))),
                  pl.BlockSpec((None, None, s, ATT_DIM), lambda i: kvmap(score_tile(i))),
                  pl.BlockSpec((None, None, ATT_DIM, s), lambda i: kvmap(softmax_tile(i)))],
        out_specs=pl.BlockSpec((None, tq, gw), lambda i: qmap(softmax_tile(i))),
        out_shape=jax.ShapeDtypeStruct((b, s, w), BF16),
        scratch_shapes=[pltpu.VMEM((n_units, s, uq), F32), pltpu.VMEM((n_units, s, uq), F32),
                        pltpu.VMEM((n_units, 1, uq), F32), pltpu.VMEM((n_units, 1, uq), F32)],
        compiler_params=_params("arbitrary"),
        name="attention",
    )(aq, ak, avt)


def _gelu2(x):
    return x + x * lax.erf(x * (2.0 ** -0.5))


def _odd_kernel(x_ref, g_ref, wi_ref, lg_ref, lb_ref, ws_ref, bs_ref, wo_ref, o_ref):
    half = wo_ref.shape[0]
    gw = half // GMLP_GROUPS
    pair = 2 * gw
    tm = x_ref.shape[0]
    sub = min(ODD_SUB, tm)
    blocks = [slice(r0, r0 + sub) for r0 in range(0, tm, sub)]

    hbs, vns = [], []
    for rows in blocks:
        hb = (_rms(x_ref[rows, :]) * g_ref[...]).astype(BF16)
        v2 = _gelu2(_dot(hb, wi_ref[:, half:]))
        mu = jnp.mean(v2, axis=-1, keepdims=True)
        vc = v2 - mu
        var = jnp.mean(vc * vc, axis=-1, keepdims=True)
        hbs.append(hb)
        vns.append((vc * lax.rsqrt(var + 4.0 * EPS) * lg_ref[...] + lb_ref[...]).astype(BF16))

    for rows, hb, vn in zip(blocks, hbs, vns):
        gated = []
        for c0 in range(0, half, pair):
            u = _gelu2(_dot(hb, wi_ref[:, c0:c0 + pair]))
            mixed = jnp.concatenate(
                [jnp.concatenate(
                    [_dot(ws_ref[g], vn[n * CHUNK:(n + 1) * CHUNK, g * gw:(g + 1) * gw]) + bs_ref[g]
                     for n in range(sub // CHUNK)], axis=0)
                 for g in (c0 // gw, c0 // gw + 1)], axis=1)
            gated.append((u * mixed).astype(BF16))
        o_ref[rows, :] = x_ref[rows, :] + _dot(jnp.concatenate(gated, axis=1), wo_ref[...])


def _odd(x2, gain, w_in, ln_g, ln_b, w_s, b_s, w_out):
    t, d = x2.shape
    tm = min(ODD_TM, t)
    half = w_out.shape[0]
    gw = half // GMLP_GROUPS
    assert w_in.shape == (d, 2 * half) and w_s.shape == (GMLP_GROUPS, CHUNK, CHUNK)
    bs_full = jnp.broadcast_to(b_s[:, :, None], (GMLP_GROUPS, CHUNK, gw))
    row = pl.BlockSpec((tm, d), lambda i: (i, 0))
    return pl.pallas_call(
        _odd_kernel,
        grid=(t // tm,),
        in_specs=[row, _resident((1, d)), _resident((d, 2 * half)), _resident((1, half)),
                  _resident((1, half)), _resident((GMLP_GROUPS, CHUNK, CHUNK)),
                  _resident((GMLP_GROUPS, CHUNK, gw)), _resident((half, d))],
        out_specs=row,
        out_shape=jax.ShapeDtypeStruct((t, d), F32),
        compiler_params=_params("parallel"),
        name="odd_mixer",
    )(x2, gain.reshape(1, d), w_in.astype(BF16), ln_g.reshape(1, half), ln_b.reshape(1, half),
      w_s.astype(BF16), bs_full, (0.5 * w_out).astype(BF16))


def kernel(x, ln_ffn1, ffn1_w_gate, ffn1_w_up, ffn1_w_down, ln_mix, even_w_in, ret_decay_fwd, ret_decay_bwd, att_q_norm, att_k_norm, even_w_out, odd_w_in, sgu_ln_g, sgu_ln_b, sgu_w_s, sgu_b_s, odd_w_out, ln_ffn2, ffn2_w_gate, ffn2_w_up, ffn2_w_down, final_norm):
    b, s, d = x.shape
    depth = ln_ffn1.shape[0]
    t = b * s
    x2 = x.reshape(t, d)
    for l in range(depth):
        x2 = _ffn(x2, ln_ffn1[l], ffn1_w_gate, ffn1_w_up, ffn1_w_down, l)
        mixer_out = None
        if l % 2 == 0:
            e = l // 2
            rq, rk, rv, rg, aq, ak, avt = _even_in(x2.reshape(b, s, d), ln_mix[l], even_w_in[e],
                                                   att_q_norm[e], att_k_norm[e])
            ret = _retention(rq, rk, rv, rg, ret_decay_fwd[e], ret_decay_bwd[e])
            att = _attention(aq, ak, avt)
            mixer_out = (ret.reshape(t, -1), att.reshape(t, -1), even_w_out[e])
        else:
            o = l // 2
            x2 = _odd(x2, ln_mix[l], odd_w_in[o], sgu_ln_g[o], sgu_ln_b[o], sgu_w_s[o],
                      sgu_b_s[o], odd_w_out[o])
        x2 = _ffn(x2, ln_ffn2[l], ffn2_w_gate, ffn2_w_up, ffn2_w_down, l, mixer_out=mixer_out,
                  final_gain=final_norm if l == depth - 1 else None)
    return x2.reshape(b, s, d)
```

```python
import functools

import numpy as np
import jax
import jax.numpy as jnp
from jax import lax
from jax.experimental import pallas as pl
from jax.experimental.pallas import tpu as pltpu

F32 = jnp.float32
BF16 = jnp.bfloat16

EPS = 1e-6
GRID_W = 64
CHUNK = 128
ROPE_THETA = 10000.0
RET_HEADS = 4
RET_DIM = 128
RET_HEADS_PER_STEP = 4
ATT_Q_HEADS = 8
ATT_KV_HEADS = 2
ATT_DIM = 64
ATT_GROUP = ATT_Q_HEADS // ATT_KV_HEADS
ATT_Q_SCALE = ATT_DIM ** -0.5 * float(np.log2(np.e))
GMLP_GROUPS = 8

LANES = 128
VMEM_LIMIT_BYTES = 56 * 1024 * 1024

FFN_TM = 1024
STAGE_ROWS = 256
FFN_SUB = 256
EVEN_TM = 1024
EVEN_SUB = 256
ODD_TM = 1024
ODD_SUB = 256
ATT_TQ = 512
ATT_UNIT_Q = 256
ATT_KEY_CHUNK = 512
MXU_N = 256


def _params(*sem):
    return pltpu.CompilerParams(dimension_semantics=sem, vmem_limit_bytes=VMEM_LIMIT_BYTES)


def _resident(shape):
    nd = len(shape)
    return pl.BlockSpec(shape, lambda *_: (0,) * nd, pipeline_mode=pl.Buffered(1))


def _rms(x):
    return x * lax.rsqrt(jnp.mean(x * x, axis=-1, keepdims=True) + EPS)


def _sigmoid(x):
    return 1.0 / (1.0 + jnp.exp(-x))


def _dot(a, b):
    return jnp.dot(a, b, preferred_element_type=F32)


def _col_chunks(n, step):
    return tuple((c, min(c + step, n)) for c in range(0, n, step))


def _stage_weights(pairs, stage_ref, sem):
    jobs = [(src, dst, r0) for src, dst in pairs for r0 in range(0, dst.shape[0], STAGE_ROWS)]

    def copy(j):
        src, dst, r0 = jobs[j]
        slot = j % 2
        return pltpu.make_async_copy(src.at[pl.ds(r0, STAGE_ROWS), :],
                                     stage_ref.at[slot, :, pl.ds(0, dst.shape[1])], sem.at[slot])

    copy(0).start()
    for j, (src, dst, r0) in enumerate(jobs):
        if j + 1 < len(jobs):
            copy(j + 1).start()
        copy(j).wait()
        dst[pl.ds(r0, STAGE_ROWS), :] = stage_ref[j % 2, :, :dst.shape[1]].astype(BF16)


def _ffn_kernel(*refs, layer, ff_chunks, mixer_out, final):
    wg_ref, wu_ref, wd_ref, stage_ref, sem = refs[-5:]
    x_ref, g_ref, wg_hbm, wu_hbm, wd_hbm = refs[:5]
    extra = list(refs[5:-6])
    o_ref = refs[-6]
    if mixer_out:
        r_ref, a_ref, wm_ref = extra[:3]
        extra = extra[3:]
        rw = r_ref.shape[1]
    if final:
        fg_ref, = extra

    @pl.when(pl.program_id(0) == 0)
    def _():
        _stage_weights([(wg_hbm.at[layer], wg_ref), (wu_hbm.at[layer], wu_ref),
                        (wd_hbm.at[layer], wd_ref)], stage_ref, sem)

    tm = x_ref.shape[0]
    sub = min(FFN_SUB, tm)
    if mixer_out:
        xm = x_ref[...] + _dot(r_ref[...], wm_ref[:rw, :]) + _dot(a_ref[...], wm_ref[rw:, :])
    for r0 in range(0, tm, sub):
        rows = slice(r0, r0 + sub)
        x = xm[rows, :] if mixer_out else x_ref[rows, :]
        hb = (_rms(x) * g_ref[...]).astype(BF16)
        acc = None
        for c0, c1 in ff_chunks:
            gate = _dot(hb, wg_ref[:, c0:c1])
            up = _dot(hb, wu_ref[:, c0:c1])
            a = (gate * _sigmoid(gate) * up).astype(BF16)
            d = _dot(a, wd_ref[c0:c1, :])
            acc = d if acc is None else acc + d
        y = x + 0.5 * acc
        if final:
            y = _rms(y) * fg_ref[...]
        o_ref[rows, :] = y


def _ffn(x2, gain, wg_all, wu_all, wd_all, layer, mixer_out=None, final_gain=None):
    t, d = x2.shape
    ff = wg_all.shape[2]
    tm = min(FFN_TM, t)
    assert d % STAGE_ROWS == 0 and ff % STAGE_ROWS == 0

    def row(w):
        return pl.BlockSpec((tm, w), lambda i: (i, 0))

    hbm = pl.BlockSpec(memory_space=pl.ANY)
    in_specs = [row(d), _resident((1, d)), hbm, hbm, hbm]
    args = [x2, gain.reshape(1, d), wg_all, wu_all, wd_all]
    if mixer_out is not None:
        ret2, att2, w_mix = mixer_out
        in_specs += [row(ret2.shape[1]), row(att2.shape[1]), _resident(w_mix.shape)]
        args += [ret2, att2, w_mix.astype(BF16)]
    if final_gain is not None:
        in_specs.append(_resident((1, d)))
        args.append(final_gain.reshape(1, d))
    name = "ffn" + ("_mix" if mixer_out is not None else "") + ("_final" if final_gain is not None else "")
    return pl.pallas_call(
        functools.partial(_ffn_kernel, layer=layer, ff_chunks=_col_chunks(ff, 4 * MXU_N),
                          mixer_out=mixer_out is not None, final=final_gain is not None),
        grid=(t // tm,),
        in_specs=in_specs,
        out_specs=row(d),
        out_shape=jax.ShapeDtypeStruct((t, d), F32),
        scratch_shapes=[pltpu.VMEM((d, ff), BF16), pltpu.VMEM((d, ff), BF16), pltpu.VMEM((ff, d), BF16),
                        pltpu.VMEM((2, STAGE_ROWS, max(d, ff)), F32), pltpu.SemaphoreType.DMA((2,))],
        compiler_params=_params("arbitrary"),
        name=name,
    )(*args)


def _rope(xb, cos, sin_signed, even_lane):
    nxt = pltpu.roll(xb, LANES - 1, 1)
    prv = pltpu.roll(xb, 1, 1)
    return xb * cos + jnp.where(even_lane, nxt, prv) * sin_signed


def _group_mean_sq(x, e_ref):
    sq = x * x
    hi = sq.astype(BF16)
    lo = (sq - hi.astype(F32)).astype(BF16)
    n = x.shape[1]
    e = e_ref[:n, :n]
    return _dot(hi, e) + _dot(lo, e)


def _even_in_kernel(x_ref, g_ref, w_ref, e_ref, cr_ref, sr_ref, ca_ref, sa_ref, gq_ref, gk_ref,
                    rq_ref, rk_ref, rv_ref, rg_ref, aq_ref, ak_ref, avt_ref, *, splits):
    o_rq, o_rk, o_rv, o_rg, o_aq, o_ak, o_av, o_end = splits
    tm = x_ref.shape[0]
    sub = min(EVEN_SUB, tm)
    even_lane = (lax.broadcasted_iota(jnp.int32, (sub, LANES), 1) % 2) == 0
    for r0 in range(0, tm, sub):
        rows = slice(r0, r0 + sub)
        hb = (_rms(x_ref[rows, :]) * g_ref[...]).astype(BF16)
        cr, sr = cr_ref[rows, :], sr_ref[rows, :]
        ca, sa = ca_ref[rows, :], sa_ref[rows, :]

        def proj(c0, c1):
            return _dot(hb, w_ref[:, c0:c1])

        aq = proj(o_aq, o_ak)
        aq = aq * lax.rsqrt(_group_mean_sq(aq, e_ref) + EPS) * gq_ref[...]
        for j in range(aq.shape[1] // LANES):
            sl = slice(j * LANES, (j + 1) * LANES)
            aq_ref[rows, sl] = (_rope(aq[:, sl], ca, sa, even_lane) * ATT_Q_SCALE).astype(BF16)
        ak = proj(o_ak, o_av)
        ak = ak * lax.rsqrt(_group_mean_sq(ak, e_ref) + EPS) * gk_ref[...]
        ak = _rope(ak, ca, sa, even_lane).astype(BF16)
        avt = proj(o_av, o_end).T.astype(BF16)
        for h in range(ATT_KV_HEADS):
            sl = slice(h * ATT_DIM, (h + 1) * ATT_DIM)
            ak_ref[h, rows, :] = ak[:, sl]
            avt_ref[h, :, rows] = avt[sl, :]

        rq = proj(o_rq, o_rk)
        rk = proj(o_rk, o_rv)
        for h in range(RET_HEADS):
            sl = slice(h * RET_DIM, (h + 1) * RET_DIM)
            rq_ref[rows, sl] = _rope(rq[:, sl], cr, sr, even_lane).astype(BF16)
            rk_ref[rows, sl] = (_rope(rk[:, sl], cr, sr, even_lane) * RET_DIM ** -0.5).astype(BF16)
        rv_ref[rows, :] = proj(o_rv, o_rg).astype(BF16)
        rg_ref[rows, :] = proj(o_rg, o_aq)


def _rope_tables(seq, head_dim):
    rows = seq // GRID_W
    row = jnp.repeat(jnp.arange(rows), GRID_W).astype(F32)
    col = jnp.tile(jnp.arange(GRID_W), rows).astype(F32)
    axis_dim = head_dim // 2
    freqs = ROPE_THETA ** (-jnp.arange(0, axis_dim, 2, dtype=F32) / axis_dim)
    ang = jnp.concatenate([row[:, None] * freqs[None, :], col[:, None] * freqs[None, :]], axis=-1)
    cos = jnp.repeat(jnp.cos(ang), 2, axis=-1)
    sign = jnp.tile(jnp.array([-1.0, 1.0], F32), head_dim // 2)
    sin = jnp.repeat(jnp.sin(ang), 2, axis=-1) * sign[None, :]
    reps = LANES // head_dim
    return jnp.tile(cos, (1, reps)), jnp.tile(sin, (1, reps))


def _even_in(x3, gain, w_in, gq, gk):
    b, s, d = x3.shape
    tm = min(EVEN_TM, s)
    ret_w = RET_HEADS * RET_DIM
    att_w = ATT_Q_HEADS * ATT_DIM
    kv_w = ATT_KV_HEADS * ATT_DIM
    sizes = (ret_w, ret_w, ret_w, ret_w, att_w, kv_w, kv_w)
    splits = (0,) + tuple(int(v) for v in np.cumsum(sizes))
    n_in = splits[-1]
    assert w_in.shape == (d, n_in)
    cr, sr = _rope_tables(s, RET_DIM)
    ca, sa = _rope_tables(s, ATT_DIM)
    grp = np.arange(att_w) // ATT_DIM
    e = jnp.asarray((grp[:, None] == grp[None, :]).astype(np.float32) / ATT_DIM, BF16)
    gq_t = jnp.tile(gq, ATT_Q_HEADS).reshape(1, att_w)
    gk_t = jnp.tile(gk, ATT_KV_HEADS).reshape(1, kv_w)

    def row(w):
        return pl.BlockSpec((None, tm, w), lambda bi, si: (bi, si, 0))

    tab = pl.BlockSpec((tm, LANES), lambda bi, si: (si, 0))
    k_out = pl.BlockSpec((None, ATT_KV_HEADS, tm, ATT_DIM), lambda bi, si: (bi, 0, si, 0))
    vt_out = pl.BlockSpec((None, ATT_KV_HEADS, ATT_DIM, tm), lambda bi, si: (bi, 0, 0, si))
    return pl.pallas_call(
        functools.partial(_even_in_kernel, splits=splits),
        grid=(b, s // tm),
        in_specs=[row(d), _resident((1, d)), _resident((d, n_in)), _resident((att_w, att_w)),
                  tab, tab, tab, tab, _resident((1, att_w)), _resident((1, kv_w))],
        out_specs=[row(ret_w), row(ret_w), row(ret_w), row(ret_w), row(att_w), k_out, vt_out],
        out_shape=[jax.ShapeDtypeStruct((b, s, ret_w), BF16),
                   jax.ShapeDtypeStruct((b, s, ret_w), BF16),
                   jax.ShapeDtypeStruct((b, s, ret_w), BF16),
                   jax.ShapeDtypeStruct((b, s, ret_w), F32),
                   jax.ShapeDtypeStruct((b, s, att_w), BF16),
                   jax.ShapeDtypeStruct((b, ATT_KV_HEADS, s, ATT_DIM), BF16),
                   jax.ShapeDtypeStruct((b, ATT_KV_HEADS, ATT_DIM, s), BF16)],
        compiler_params=_params("parallel", "parallel"),
        name="even_in",
    )(x3, gain.reshape(1, d), w_in.astype(BF16), e, cr, sr, ca, sa, gq_t, gk_t)


def _ret_kernel(df_ref, db_ref, q_ref, k_ref, v_ref, g_ref, o_ref, st_ref):
    c = CHUNK
    d = RET_DIM
    n_chunks = q_ref.shape[0] // c
    heads = range(q_ref.shape[1] // d)
    ri = lax.broadcasted_iota(jnp.int32, (c, c), 0).astype(F32)
    ci = lax.broadcasted_iota(jnp.int32, (c, c), 1).astype(F32)
    diff = ri - ci
    tdot = functools.partial(lax.dot_general, dimension_numbers=(((0,), (0,)), ((), ())),
                             preferred_element_type=F32)
    mask, kw_f, qw_f, kw_b, qw_b, dec_f, dec_b = [], [], [], [], [], [], []
    for h in heads:
        lam_f = jnp.exp(df_ref[h])
        lam_b = jnp.exp(db_ref[h])
        mask.append(jnp.where(diff >= 0,
                              jnp.exp(-lam_f * jnp.maximum(diff, 0.0)),
                              jnp.exp(-lam_b * jnp.maximum(-diff, 0.0))))
        kw_f.append(jnp.exp(-lam_f * (c - 1.0 - ri)))
        qw_f.append(jnp.exp(-lam_f * (ri + 1.0)))
        kw_b.append(jnp.exp(-lam_b * ri))
        qw_b.append(jnp.exp(-lam_b * (c - ri)))
        dec_f.append(jnp.exp(-lam_f * c))
        dec_b.append(jnp.exp(-lam_b * c))

    def blk(ref, n, h):
        return ref[n * c:(n + 1) * c, h * d:(h + 1) * d]

    st = [jnp.zeros((c, c), F32) for _ in heads]
    for n in range(n_chunks):
        for h in heads:
            st_ref[h, n, :c, :] = st[h].astype(BF16)
            k = blk(k_ref, n, h).astype(F32)
            st[h] = dec_f[h] * st[h] + tdot((k * kw_f[h]).astype(BF16), blk(v_ref, n, h))
    st = [jnp.zeros((c, c), F32) for _ in heads]
    for n in reversed(range(n_chunks)):
        for h in heads:
            st_ref[h, n, c:, :] = st[h].astype(BF16)
            k = blk(k_ref, n, h).astype(F32)
            st[h] = dec_b[h] * st[h] + tdot((k * kw_b[h]).astype(BF16), blk(v_ref, n, h))

    for n in range(n_chunks):
        for h in heads:
            qb = blk(q_ref, n, h)
            q = qb.astype(F32)
            s = lax.dot_general(qb, blk(k_ref, n, h), (((1,), (1,)), ((), ())),
                                preferred_element_type=F32) * mask[h]
            qq = jnp.concatenate([(q * qw_f[h]).astype(BF16), (q * qw_b[h]).astype(BF16)], axis=1)
            o = _dot(s.astype(BF16), blk(v_ref, n, h)) + _dot(qq, st_ref[h, n])
            gate = blk(g_ref, n, h)
            o_ref[n * c:(n + 1) * c, h * d:(h + 1) * d] = (
                _rms(o) * (gate * _sigmoid(gate))).astype(BF16)


def _retention(rq, rk, rv, rg, decay_f, decay_b):
    b, s, w = rq.shape
    heads = w // RET_DIM
    hp = RET_HEADS_PER_STEP
    n_chunks = s // CHUNK
    blk = pl.BlockSpec((None, s, hp * RET_DIM), lambda bi, hi: (bi, 0, hi))
    dec = pl.BlockSpec((hp, 1, LANES), lambda bi, hi: (hi, 0, 0))
    df = jnp.broadcast_to(decay_f.reshape(heads, 1, 1), (heads, 1, LANES))
    db = jnp.broadcast_to(decay_b.reshape(heads, 1, 1), (heads, 1, LANES))
    return pl.pallas_call(
        _ret_kernel,
        grid=(b, heads // hp),
        in_specs=[dec, dec, blk, blk, blk, blk],
        out_specs=blk,
        out_shape=jax.ShapeDtypeStruct((b, s, w), BF16),
        scratch_shapes=[pltpu.VMEM((hp, n_chunks, 2 * CHUNK, RET_DIM), BF16)],
        compiler_params=_params("parallel", "parallel"),
        name="retention",
    )(df, db, rq, rk, rv, rg)


def _att_kernel(q_ref, k_ref, vt_ref, o_ref, s0_ref, s1_ref, m0_ref, m1_ref):
    i = pl.program_id(0)
    d = k_ref.shape[1]
    n_units, _, uq = s0_ref.shape
    units = [(r, h) for r in range(n_units // ATT_GROUP) for h in range(ATT_GROUP)]

    @pl.when(i == 0)
    def _():
        s1_ref[...] = jnp.zeros(s1_ref.shape, F32)
        m1_ref[...] = jnp.zeros(m1_ref.shape, F32)

    def step(sw_ref, mw_ref, sr_ref, mr_ref):
        n_keys = k_ref.shape[0]
        kc = min(ATT_KEY_CHUNK, n_keys)
        outs = []
        for u0 in range(0, n_units, 2):
            pair = (u0, u0 + 1)
            qs = {u: q_ref[units[u][0] * uq:(units[u][0] + 1) * uq,
                           units[u][1] * d:(units[u][1] + 1) * d] for u in pair}
            mx = {u: None for u in pair}
            den = {u: None for u in pair}
            acc = {u: None for u in pair}
            for c0 in range(0, n_keys, kc):
                rows = slice(c0, c0 + kc)
                for u in pair:
                    st = lax.dot_general(k_ref[rows, :], qs[u], (((1,), (1,)), ((), ())),
                                         preferred_element_type=F32)
                    sw_ref[u, rows, :] = st
                    cm = jnp.max(st, axis=0, keepdims=True)
                    mx[u] = cm if mx[u] is None else jnp.maximum(mx[u], cm)
                for u in pair:
                    p = jnp.exp2(sr_ref[u, rows, :] - mr_ref[u])
                    ps = jnp.sum(p, axis=0, keepdims=True)
                    pv = _dot(vt_ref[:, rows], p.astype(BF16))
                    den[u] = ps if den[u] is None else den[u] + ps
                    acc[u] = pv if acc[u] is None else acc[u] + pv
            for u in pair:
                mw_ref[u] = mx[u]
                outs.append(acc[u] * (1.0 / den[u]))
            if len(outs) == ATT_GROUP:
                r = units[u0][0]
                o_ref[r * uq:(r + 1) * uq, :] = jnp.concatenate(outs, axis=0).T.astype(BF16)
                outs = []

    @pl.when(i % 2 == 0)
    def _():
        step(s0_ref, m0_ref, s1_ref, m1_ref)

    @pl.when(i % 2 == 1)
    def _():
        step(s1_ref, m1_ref, s0_ref, m0_ref)


def _attention(aq, ak, avt):
    b, s, w = aq.shape
    tq = min(ATT_TQ, s)
    uq = min(ATT_UNIT_Q, tq)
    gw = ATT_GROUP * ATT_DIM
    nq = s // tq
    n_tiles = b * ATT_KV_HEADS * nq
    n_units = ATT_GROUP * (tq // uq)

    def tile(t):
        return t // (ATT_KV_HEADS * nq), (t // nq) % ATT_KV_HEADS, t % nq

    def score_tile(i):
        return tile(jnp.minimum(i, n_tiles - 1))

    def softmax_tile(i):
        return tile(jnp.maximum(i - 1, 0))

    def qmap(bgq):
        bi, gi, qi = bgq
        return bi, qi, gi

    def kvmap(bgq):
        bi, gi, _ = bgq
        return bi, gi, 0, 0

    return pl.pallas_call(
        _att_kernel,
        grid=(n_tiles + 1,),
        in_specs=[pl.BlockSpec((None, tq, gw), lambda i: qmap(score_tile(i))),
                  pl.BlockSpec((None, None, s, ATT_DIM), lambda i: kvmap(score_tile(i))),
                  pl.BlockSpec((None, None, ATT_DIM, s), lambda i: kvmap(softmax_tile(i)))],
        out_specs=pl.BlockSpec((None, tq, gw), lambda i: qmap(softmax_tile(i))),
        out_shape=jax.ShapeDtypeStruct((b, s, w), BF16),
        scratch_shapes=[pltpu.VMEM((n_units, s, uq), F32), pltpu.VMEM((n_units, s, uq), F32),
                        pltpu.VMEM((n_units, 1, uq), F32), pltpu.VMEM((n_units, 1, uq), F32)],
        compiler_params=_params("arbitrary"),
        name="attention",
    )(aq, ak, avt)


def _gelu2(x):
    return x + x * lax.erf(x * (2.0 ** -0.5))


def _spatial_mix(w, bias, vg):
    n = vg.shape[0] // CHUNK
    gw = vg.shape[1]
    wide = _dot(w, jnp.concatenate([vg[i * CHUNK:(i + 1) * CHUNK, :] for i in range(n)], axis=1))
    return jnp.concatenate([wide[:, i * gw:(i + 1) * gw] + bias for i in range(n)], axis=0)


def _odd_kernel(x_ref, g_ref, wi_ref, lg_ref, lb_ref, ws_ref, bs_ref, wo_ref, o_ref):
    half = wo_ref.shape[0]
    gw = half // GMLP_GROUPS
    pair = 2 * gw
    tm = x_ref.shape[0]
    sub = min(ODD_SUB, tm)
    blocks = [slice(r0, r0 + sub) for r0 in range(0, tm, sub)]

    hbs, vns = [], []
    for rows in blocks:
        hb = (_rms(x_ref[rows, :]) * g_ref[...]).astype(BF16)
        v2 = _gelu2(_dot(hb, wi_ref[:, half:]))
        mu = jnp.mean(v2, axis=-1, keepdims=True)
        vc = v2 - mu
        var = jnp.mean(vc * vc, axis=-1, keepdims=True)
        hbs.append(hb)
        vns.append((vc * lax.rsqrt(var + 4.0 * EPS) * lg_ref[...] + lb_ref[...]).astype(BF16))

    for rows, hb, vn in zip(blocks, hbs, vns):
        gated = []
        for c0 in range(0, half, pair):
            u = _gelu2(_dot(hb, wi_ref[:, c0:c0 + pair]))
            mixed = jnp.concatenate([_spatial_mix(ws_ref[g], bs_ref[g], vn[:, g * gw:(g + 1) * gw])
                                     for g in (c0 // gw, c0 // gw + 1)], axis=1)
            gated.append((u * mixed).astype(BF16))
        o_ref[rows, :] = x_ref[rows, :] + _dot(jnp.concatenate(gated, axis=1), wo_ref[...])


def _odd(x2, gain, w_in, ln_g, ln_b, w_s, b_s, w_out):
    t, d = x2.shape
    tm = min(ODD_TM, t)
    half = w_out.shape[0]
    gw = half // GMLP_GROUPS
    assert w_in.shape == (d, 2 * half) and w_s.shape == (GMLP_GROUPS, CHUNK, CHUNK)
    bs_full = jnp.broadcast_to(b_s[:, :, None], (GMLP_GROUPS, CHUNK, gw))
    row = pl.BlockSpec((tm, d), lambda i: (i, 0))
    return pl.pallas_call(
        _odd_kernel,
        grid=(t // tm,),
        in_specs=[row, _resident((1, d)), _resident((d, 2 * half)), _resident((1, half)),
                  _resident((1, half)), _resident((GMLP_GROUPS, CHUNK, CHUNK)),
                  _resident((GMLP_GROUPS, CHUNK, gw)), _resident((half, d))],
        out_specs=row,
        out_shape=jax.ShapeDtypeStruct((t, d), F32),
        compiler_params=_params("parallel"),
        name="odd_mixer",
    )(x2, gain.reshape(1, d), w_in.astype(BF16), ln_g.reshape(1, half), ln_b.reshape(1, half),
      w_s.astype(BF16), bs_full, (0.5 * w_out).astype(BF16))


def kernel(x, ln_ffn1, ffn1_w_gate, ffn1_w_up, ffn1_w_down, ln_mix, even_w_in, ret_decay_fwd, ret_decay_bwd, att_q_norm, att_k_norm, even_w_out, odd_w_in, sgu_ln_g, sgu_ln_b, sgu_w_s, sgu_b_s, odd_w_out, ln_ffn2, ffn2_w_gate, ffn2_w_up, ffn2_w_down, final_norm):
    b, s, d = x.shape
    depth = ln_ffn1.shape[0]
    t = b * s
    x2 = x.reshape(t, d)
    for l in range(depth):
        x2 = _ffn(x2, ln_ffn1[l], ffn1_w_gate, ffn1_w_up, ffn1_w_down, l)
        mixer_out = None
        if l % 2 == 0:
            e = l // 2
            rq, rk, rv, rg, aq, ak, avt = _even_in(x2.reshape(b, s, d), ln_mix[l], even_w_in[e],
                                                   att_q_norm[e], att_k_norm[e])
            ret = _retention(rq, rk, rv, rg, ret_decay_fwd[e], ret_decay_bwd[e])
            att = _attention(aq, ak, avt)
            mixer_out = (ret.reshape(t, -1), att.reshape(t, -1), even_w_out[e])
        else:
            o = l // 2
            x2 = _odd(x2, ln_mix[l], odd_w_in[o], sgu_ln_g[o], sgu_ln_b[o], sgu_w_s[o],
                      sgu_b_s[o], odd_w_out[o])
        x2 = _ffn(x2, ln_ffn2[l], ffn2_w_gate, ffn2_w_up, ffn2_w_down, l, mixer_out=mixer_out,
                  final_gain=final_norm if l == depth - 1 else None)
    return x2.reshape(b, s, d)
```

```python
import functools

import numpy as np
import jax
import jax.numpy as jnp
from jax import lax
from jax.experimental import pallas as pl
from jax.experimental.pallas import tpu as pltpu

F32 = jnp.float32
BF16 = jnp.bfloat16

EPS = 1e-6
GRID_W = 64
CHUNK = 128
ROPE_THETA = 10000.0
RET_HEADS = 4
RET_DIM = 128
RET_HEADS_PER_STEP = 4
ATT_Q_HEADS = 8
ATT_KV_HEADS = 2
ATT_DIM = 64
ATT_GROUP = ATT_Q_HEADS // ATT_KV_HEADS
ATT_Q_SCALE = ATT_DIM ** -0.5 * float(np.log2(np.e))
GMLP_GROUPS = 8

LANES = 128
VMEM_LIMIT_BYTES = 56 * 1024 * 1024

FFN_TM = 1024
STAGE_ROWS = 256
FFN_SUB = 256
EVEN_TM = 1024
EVEN_SUB = 256
ODD_TM = 1024
ODD_SUB = 256
ATT_TQ = 512
ATT_UNIT_Q = 256
ATT_KEY_CHUNK = 512
MXU_N = 256


def _params(*sem):
    return pltpu.CompilerParams(dimension_semantics=sem, vmem_limit_bytes=VMEM_LIMIT_BYTES)


def _resident(shape):
    nd = len(shape)
    return pl.BlockSpec(shape, lambda *_: (0,) * nd, pipeline_mode=pl.Buffered(1))


def _rms(x):
    return x * lax.rsqrt(jnp.mean(x * x, axis=-1, keepdims=True) + EPS)


def _sigmoid(x):
    return 1.0 / (1.0 + jnp.exp(-x))


def _dot(a, b):
    return jnp.dot(a, b, preferred_element_type=F32)


def _col_chunks(n, step):
    return tuple((c, min(c + step, n)) for c in range(0, n, step))


def _stage_weights(pairs, stage_ref, sem):
    jobs = [(src, dst, r0) for src, dst in pairs for r0 in range(0, dst.shape[0], STAGE_ROWS)]

    def copy(j):
        src, dst, r0 = jobs[j]
        slot = j % 2
        return pltpu.make_async_copy(src.at[pl.ds(r0, STAGE_ROWS), :],
                                     stage_ref.at[slot, :, pl.ds(0, dst.shape[1])], sem.at[slot])

    copy(0).start()
    for j, (src, dst, r0) in enumerate(jobs):
        if j + 1 < len(jobs):
            copy(j + 1).start()
        copy(j).wait()
        dst[pl.ds(r0, STAGE_ROWS), :] = stage_ref[j % 2, :, :dst.shape[1]].astype(BF16)


def _ffn_kernel(*refs, layer, ff_chunks, mixer_out, final):
    wg_ref, wu_ref, wd_ref, stage_ref, sem = refs[-5:]
    x_ref, g_ref, wg_hbm, wu_hbm, wd_hbm = refs[:5]
    extra = list(refs[5:-6])
    o_ref = refs[-6]
    if mixer_out:
        r_ref, a_ref, wm_ref = extra[:3]
        extra = extra[3:]
        rw = r_ref.shape[1]
    if final:
        fg_ref, = extra

    @pl.when(pl.program_id(0) == 0)
    def _():
        _stage_weights([(wg_hbm.at[layer], wg_ref), (wu_hbm.at[layer], wu_ref),
                        (wd_hbm.at[layer], wd_ref)], stage_ref, sem)

    tm = x_ref.shape[0]
    sub = min(FFN_SUB, tm)
    if mixer_out:
        xm = x_ref[...] + _dot(r_ref[...], wm_ref[:rw, :]) + _dot(a_ref[...], wm_ref[rw:, :])
    for r0 in range(0, tm, sub):
        rows = slice(r0, r0 + sub)
        x = xm[rows, :] if mixer_out else x_ref[rows, :]
        hb = (_rms(x) * g_ref[...]).astype(BF16)
        acc = None
        for c0, c1 in ff_chunks:
            gate = _dot(hb, wg_ref[:, c0:c1])
            up = _dot(hb, wu_ref[:, c0:c1])
            a = (gate * _sigmoid(gate) * up).astype(BF16)
            d = _dot(a, wd_ref[c0:c1, :])
            acc = d if acc is None else acc + d
        y = x + 0.5 * acc
        if final:
            y = _rms(y) * fg_ref[...]
        o_ref[rows, :] = y


def _ffn(x2, gain, wg_all, wu_all, wd_all, layer, mixer_out=None, final_gain=None):
    t, d = x2.shape
    ff = wg_all.shape[2]
    tm = min(FFN_TM, t)
    assert d % STAGE_ROWS == 0 and ff % STAGE_ROWS == 0

    def row(w):
        return pl.BlockSpec((tm, w), lambda i: (i, 0))

    hbm = pl.BlockSpec(memory_space=pl.ANY)
    in_specs = [row(d), _resident((1, d)), hbm, hbm, hbm]
    args = [x2, gain.reshape(1, d), wg_all, wu_all, wd_all]
    if mixer_out is not None:
        ret2, att2, w_mix = mixer_out
        in_specs += [row(ret2.shape[1]), row(att2.shape[1]), _resident(w_mix.shape)]
        args += [ret2, att2, w_mix.astype(BF16)]
    if final_gain is not None:
        in_specs.append(_resident((1, d)))
        args.append(final_gain.reshape(1, d))
    name = "ffn" + ("_mix" if mixer_out is not None else "") + ("_final" if final_gain is not None else "")
    return pl.pallas_call(
        functools.partial(_ffn_kernel, layer=layer, ff_chunks=_col_chunks(ff, 4 * MXU_N),
                          mixer_out=mixer_out is not None, final=final_gain is not None),
        grid=(t // tm,),
        in_specs=in_specs,
        out_specs=row(d),
        out_shape=jax.ShapeDtypeStruct((t, d), F32),
        scratch_shapes=[pltpu.VMEM((d, ff), BF16), pltpu.VMEM((d, ff), BF16), pltpu.VMEM((ff, d), BF16),
                        pltpu.VMEM((2, STAGE_ROWS, max(d, ff)), F32), pltpu.SemaphoreType.DMA((2,))],
        compiler_params=_params("arbitrary"),
        name=name,
    )(*args)


def _rope(xb, cos, sin_signed, even_lane):
    nxt = pltpu.roll(xb, LANES - 1, 1)
    prv = pltpu.roll(xb, 1, 1)
    return xb * cos + jnp.where(even_lane, nxt, prv) * sin_signed


def _group_mean_sq(x, e_ref):
    sq = x * x
    hi = sq.astype(BF16)
    lo = (sq - hi.astype(F32)).astype(BF16)
    n = x.shape[1]
    w = min(MXU_N, n)
    e = e_ref[:w, :w]
    return jnp.concatenate([_dot(hi[:, c:c + w], e) + _dot(lo[:, c:c + w], e)
                            for c in range(0, n, w)], axis=1)


def _even_in_kernel(x_ref, g_ref, w_ref, e_ref, cr_ref, sr_ref, ca_ref, sa_ref, gq_ref, gk_ref,
                    rq_ref, rk_ref, rv_ref, rg_ref, aq_ref, ak_ref, avt_ref, *, splits):
    o_rq, o_rk, o_rv, o_rg, o_aq, o_ak, o_av, o_end = splits
    tm = x_ref.shape[0]
    sub = min(EVEN_SUB, tm)
    even_lane = (lax.broadcasted_iota(jnp.int32, (sub, LANES), 1) % 2) == 0
    for r0 in range(0, tm, sub):
        rows = slice(r0, r0 + sub)
        hb = (_rms(x_ref[rows, :]) * g_ref[...]).astype(BF16)
        cr, sr = cr_ref[rows, :], sr_ref[rows, :]
        ca, sa = ca_ref[rows, :], sa_ref[rows, :]

        def proj(c0, c1):
            return _dot(hb, w_ref[:, c0:c1])

        aq = proj(o_aq, o_ak)
        ak = proj(o_ak, o_av)
        rq = proj(o_rq, o_rk)
        rk = proj(o_rk, o_rv)
        aq = aq * lax.rsqrt(_group_mean_sq(aq, e_ref) + EPS) * gq_ref[...]
        for j in range(aq.shape[1] // LANES):
            sl = slice(j * LANES, (j + 1) * LANES)
            aq_ref[rows, sl] = (_rope(aq[:, sl], ca, sa, even_lane) * ATT_Q_SCALE).astype(BF16)
        ak = ak * lax.rsqrt(_group_mean_sq(ak, e_ref) + EPS) * gk_ref[...]
        ak = _rope(ak, ca, sa, even_lane).astype(BF16)
        avt = proj(o_av, o_end).T.astype(BF16)
        for h in range(ATT_KV_HEADS):
            sl = slice(h * ATT_DIM, (h + 1) * ATT_DIM)
            ak_ref[h, rows, :] = ak[:, sl]
            avt_ref[h, :, rows] = avt[sl, :]

        for h in range(RET_HEADS):
            sl = slice(h * RET_DIM, (h + 1) * RET_DIM)
            rq_ref[rows, sl] = _rope(rq[:, sl], cr, sr, even_lane).astype(BF16)
            rk_ref[rows, sl] = (_rope(rk[:, sl], cr, sr, even_lane) * RET_DIM ** -0.5).astype(BF16)
        rv_ref[rows, :] = proj(o_rv, o_rg).astype(BF16)
        rg_ref[rows, :] = proj(o_rg, o_aq)


def _rope_tables(seq, head_dim):
    rows = seq // GRID_W
    row = jnp.repeat(jnp.arange(rows), GRID_W).astype(F32)
    col = jnp.tile(jnp.arange(GRID_W), rows).astype(F32)
    axis_dim = head_dim // 2
    freqs = ROPE_THETA ** (-jnp.arange(0, axis_dim, 2, dtype=F32) / axis_dim)
    ang = jnp.concatenate([row[:, None] * freqs[None, :], col[:, None] * freqs[None, :]], axis=-1)
    cos = jnp.repeat(jnp.cos(ang), 2, axis=-1)
    sign = jnp.tile(jnp.array([-1.0, 1.0], F32), head_dim // 2)
    sin = jnp.repeat(jnp.sin(ang), 2, axis=-1) * sign[None, :]
    reps = LANES // head_dim
    return jnp.tile(cos, (1, reps)), jnp.tile(sin, (1, reps))


def _even_in(x3, gain, w_in, gq, gk):
    b, s, d = x3.shape
    tm = min(EVEN_TM, s)
    ret_w = RET_HEADS * RET_DIM
    att_w = ATT_Q_HEADS * ATT_DIM
    kv_w = ATT_KV_HEADS * ATT_DIM
    sizes = (ret_w, ret_w, ret_w, ret_w, att_w, kv_w, kv_w)
    splits = (0,) + tuple(int(v) for v in np.cumsum(sizes))
    n_in = splits[-1]
    assert w_in.shape == (d, n_in)
    cr, sr = _rope_tables(s, RET_DIM)
    ca, sa = _rope_tables(s, ATT_DIM)
    grp = np.arange(MXU_N) // ATT_DIM
    e = jnp.asarray((grp[:, None] == grp[None, :]).astype(np.float32) / ATT_DIM, BF16)
    gq_t = jnp.tile(gq, ATT_Q_HEADS).reshape(1, att_w)
    gk_t = jnp.tile(gk, ATT_KV_HEADS).reshape(1, kv_w)

    def row(w):
        return pl.BlockSpec((None, tm, w), lambda bi, si: (bi, si, 0))

    tab = pl.BlockSpec((tm, LANES), lambda bi, si: (si, 0))
    k_out = pl.BlockSpec((None, ATT_KV_HEADS, tm, ATT_DIM), lambda bi, si: (bi, 0, si, 0))
    vt_out = pl.BlockSpec((None, ATT_KV_HEADS, ATT_DIM, tm), lambda bi, si: (bi, 0, 0, si))
    return pl.pallas_call(
        functools.partial(_even_in_kernel, splits=splits),
        grid=(b, s // tm),
        in_specs=[row(d), _resident((1, d)), _resident((d, n_in)), _resident((MXU_N, MXU_N)),
                  tab, tab, tab, tab, _resident((1, att_w)), _resident((1, kv_w))],
        out_specs=[row(ret_w), row(ret_w), row(ret_w), row(ret_w), row(att_w), k_out, vt_out],
        out_shape=[jax.ShapeDtypeStruct((b, s, ret_w), BF16),
                   jax.ShapeDtypeStruct((b, s, ret_w), BF16),
                   jax.ShapeDtypeStruct((b, s, ret_w), BF16),
                   jax.ShapeDtypeStruct((b, s, ret_w), F32),
                   jax.ShapeDtypeStruct((b, s, att_w), BF16),
                   jax.ShapeDtypeStruct((b, ATT_KV_HEADS, s, ATT_DIM), BF16),
                   jax.ShapeDtypeStruct((b, ATT_KV_HEADS, ATT_DIM, s), BF16)],
        compiler_params=_params("parallel", "parallel"),
        name="even_in",
    )(x3, gain.reshape(1, d), w_in.astype(BF16), e, cr, sr, ca, sa, gq_t, gk_t)


def _ret_kernel(df_ref, db_ref, q_ref, k_ref, v_ref, g_ref, o_ref, st_ref):
    c = CHUNK
    d = RET_DIM
    n_chunks = q_ref.shape[0] // c
    heads = range(q_ref.shape[1] // d)
    ri = lax.broadcasted_iota(jnp.int32, (c, c), 0).astype(F32)
    ci = lax.broadcasted_iota(jnp.int32, (c, c), 1).astype(F32)
    diff = ri - ci
    tdot = functools.partial(lax.dot_general, dimension_numbers=(((0,), (0,)), ((), ())),
                             preferred_element_type=F32)
    mask, kw_f, qw_f, kw_b, qw_b, dec_f, dec_b = [], [], [], [], [], [], []
    for h in heads:
        lam_f = jnp.exp(df_ref[h])
        lam_b = jnp.exp(db_ref[h])
        mask.append(jnp.where(diff >= 0,
                              jnp.exp(-lam_f * jnp.maximum(diff, 0.0)),
                              jnp.exp(-lam_b * jnp.maximum(-diff, 0.0))))
        kw_f.append(jnp.exp(-lam_f * (c - 1.0 - ri)))
        qw_f.append(jnp.exp(-lam_f * (ri + 1.0)))
        kw_b.append(jnp.exp(-lam_b * ri))
        qw_b.append(jnp.exp(-lam_b * (c - ri)))
        dec_f.append(jnp.exp(-lam_f * c))
        dec_b.append(jnp.exp(-lam_b * c))

    def blk(ref, n, h):
        return ref[n * c:(n + 1) * c, h * d:(h + 1) * d]

    st = [jnp.zeros((c, c), F32) for _ in heads]
    for n in range(n_chunks):
        for h in heads:
            st_ref[h, n, :c, :] = st[h].astype(BF16)
            k = blk(k_ref, n, h).astype(F32)
            st[h] = dec_f[h] * st[h] + tdot((k * kw_f[h]).astype(BF16), blk(v_ref, n, h))
    st = [jnp.zeros((c, c), F32) for _ in heads]
    for n in reversed(range(n_chunks)):
        for h in heads:
            st_ref[h, n, c:, :] = st[h].astype(BF16)
            k = blk(k_ref, n, h).astype(F32)
            st[h] = dec_b[h] * st[h] + tdot((k * kw_b[h]).astype(BF16), blk(v_ref, n, h))

    for n in range(n_chunks):
        for h in heads:
            qb = blk(q_ref, n, h)
            q = qb.astype(F32)
            s = lax.dot_general(qb, blk(k_ref, n, h), (((1,), (1,)), ((), ())),
                                preferred_element_type=F32) * mask[h]
            qq = jnp.concatenate([(q * qw_f[h]).astype(BF16), (q * qw_b[h]).astype(BF16)], axis=1)
            o = _dot(s.astype(BF16), blk(v_ref, n, h)) + _dot(qq, st_ref[h, n])
            gate = blk(g_ref, n, h)
            o_ref[n * c:(n + 1) * c, h * d:(h + 1) * d] = (
                _rms(o) * (gate * _sigmoid(gate))).astype(BF16)


def _retention(rq, rk, rv, rg, decay_f, decay_b):
    b, s, w = rq.shape
    heads = w // RET_DIM
    hp = RET_HEADS_PER_STEP
    n_chunks = s // CHUNK
    blk = pl.BlockSpec((None, s, hp * RET_DIM), lambda bi, hi: (bi, 0, hi))
    dec = pl.BlockSpec((hp, 1, LANES), lambda bi, hi: (hi, 0, 0))
    df = jnp.broadcast_to(decay_f.reshape(heads, 1, 1), (heads, 1, LANES))
    db = jnp.broadcast_to(decay_b.reshape(heads, 1, 1), (heads, 1, LANES))
    return pl.pallas_call(
        _ret_kernel,
        grid=(b, heads // hp),
        in_specs=[dec, dec, blk, blk, blk, blk],
        out_specs=blk,
        out_shape=jax.ShapeDtypeStruct((b, s, w), BF16),
        scratch_shapes=[pltpu.VMEM((hp, n_chunks, 2 * CHUNK, RET_DIM), BF16)],
        compiler_params=_params("parallel", "parallel"),
        name="retention",
    )(df, db, rq, rk, rv, rg)


def _att_kernel(q_ref, k_ref, vt_ref, o_ref, s0_ref, s1_ref, m0_ref, m1_ref):
    i = pl.program_id(0)
    d = k_ref.shape[1]
    n_units, _, uq = s0_ref.shape
    units = [(r, h) for r in range(n_units // ATT_GROUP) for h in range(ATT_GROUP)]

    @pl.when(i == 0)
    def _():
        s1_ref[...] = jnp.zeros(s1_ref.shape, F32)
        m1_ref[...] = jnp.zeros(m1_ref.shape, F32)

    def step(sw_ref, mw_ref, sr_ref, mr_ref):
        n_keys = k_ref.shape[0]
        kc = min(ATT_KEY_CHUNK, n_keys)
        outs = []
        for u0 in range(0, n_units, 2):
            pair = (u0, u0 + 1)
            qs = {u: q_ref[units[u][0] * uq:(units[u][0] + 1) * uq,
                           units[u][1] * d:(units[u][1] + 1) * d] for u in pair}
            mx = {u: None for u in pair}
            den = {u: None for u in pair}
            acc = {u: None for u in pair}
            for c0 in range(0, n_keys, kc):
                rows = slice(c0, c0 + kc)
                for u in pair:
                    st = lax.dot_general(k_ref[rows, :], qs[u], (((1,), (1,)), ((), ())),
                                         preferred_element_type=F32)
                    sw_ref[u, rows, :] = st
                    cm = jnp.max(st, axis=0, keepdims=True)
                    mx[u] = cm if mx[u] is None else jnp.maximum(mx[u], cm)
                for u in pair:
                    p = jnp.exp2(sr_ref[u, rows, :] - mr_ref[u])
                    ps = jnp.sum(p, axis=0, keepdims=True)
                    pv = _dot(vt_ref[:, rows], p.astype(BF16))
                    den[u] = ps if den[u] is None else den[u] + ps
                    acc[u] = pv if acc[u] is None else acc[u] + pv
            for u in pair:
                mw_ref[u] = mx[u]
                outs.append(acc[u] * (1.0 / den[u]))
            if len(outs) == ATT_GROUP:
                r = units[u0][0]
                o_ref[r * uq:(r + 1) * uq, :] = jnp.concatenate(outs, axis=0).T.astype(BF16)
                outs = []

    @pl.when(i % 2 == 0)
    def _():
        step(s0_ref, m0_ref, s1_ref, m1_ref)

    @pl.when(i % 2 == 1)
    def _():
        step(s1_ref, m1_ref, s0_ref, m0_ref)


def _attention(aq, ak, avt):
    b, s, w = aq.shape
    tq = min(ATT_TQ, s)
    uq = min(ATT_UNIT_Q, tq)
    gw = ATT_GROUP * ATT_DIM
    nq = s // tq
    n_tiles = b * ATT_KV_HEADS * nq
    n_units = ATT_GROUP * (tq // uq)

    def tile(t):
        return t // (ATT_KV_HEADS * nq), (t // nq) % ATT_KV_HEADS, t % nq

    def score_tile(i):
        return tile(jnp.minimum(i, n_tiles - 1))

    def softmax_tile(i):
        return tile(jnp.maximum(i - 1, 0))

    def qmap(bgq):
        bi, gi, qi = bgq
        return bi, qi, gi

    def kvmap(bgq):
        bi, gi, _ = bgq
        return bi, gi, 0, 0

    return pl.pallas_call(
        _att_kernel,
        grid=(n_tiles + 1,),
        in_specs=[pl.BlockSpec((None, tq, gw), lambda i: qmap(score_tile(i))),
                  pl.BlockSpec((None, None, s, ATT_DIM), lambda i: kvmap(score_tile(i))),
                  pl.BlockSpec((None, None, ATT_DIM, s), lambda i: kvmap(softmax_tile(i)))],
        out_specs=pl.BlockSpec((None, tq, gw), lambda i: qmap(softmax_tile(i))),
        out_shape=jax.ShapeDtypeStruct((b, s, w), BF16),
        scratch_shapes=[pltpu.VMEM((n_units, s, uq), F32), pltpu.VMEM((n_units, s, uq), F32),
                        pltpu.VMEM((n_units, 1, uq), F32), pltpu.VMEM((n_units, 1, uq), F32)],
        compiler_params=_params("arbitrary"),
        name="attention",
    )(aq, ak, avt)


def _gelu2(x):
    return x + x * lax.erf(x * (2.0 ** -0.5))


def _spatial_mix(w, bias, vg):
    n = vg.shape[0] // CHUNK
    gw = vg.shape[1]
    wide = _dot(w, jnp.concatenate([vg[i * CHUNK:(i + 1) * CHUNK, :] for i in range(n)], axis=1))
    return jnp.concatenate([wide[:, i * gw:(i + 1) * gw] + bias for i in range(n)], axis=0)


def _odd_kernel(x_ref, g_ref, wi_ref, lg_ref, lb_ref, ws_ref, bs_ref, wo_ref, o_ref):
    half = wo_ref.shape[0]
    gw = half // GMLP_GROUPS
    pair = 2 * gw
    tm = x_ref.shape[0]
    sub = min(ODD_SUB, tm)
    blocks = [slice(r0, r0 + sub) for r0 in range(0, tm, sub)]

    hbs, vns = [], []
    for rows in blocks:
        hb = (_rms(x_ref[rows, :]) * g_ref[...]).astype(BF16)
        v2 = _gelu2(_dot(hb, wi_ref[:, half:]))
        mu = jnp.mean(v2, axis=-1, keepdims=True)
        vc = v2 - mu
        var = jnp.mean(vc * vc, axis=-1, keepdims=True)
        hbs.append(hb)
        vns.append((vc * lax.rsqrt(var + 4.0 * EPS) * lg_ref[...] + lb_ref[...]).astype(BF16))

    for rows, hb, vn in zip(blocks, hbs, vns):
        gated = []
        for c0 in range(0, half, pair):
            u = _gelu2(_dot(hb, wi_ref[:, c0:c0 + pair]))
            mixed = jnp.concatenate([_spatial_mix(ws_ref[g], bs_ref[g], vn[:, g * gw:(g + 1) * gw])
                                     for g in (c0 // gw, c0 // gw + 1)], axis=1)
            gated.append((u * mixed).astype(BF16))
        o_ref[rows, :] = x_ref[rows, :] + _dot(jnp.concatenate(gated, axis=1), wo_ref[...])


def _odd(x2, gain, w_in, ln_g, ln_b, w_s, b_s, w_out):
    t, d = x2.shape
    tm = min(ODD_TM, t)
    half = w_out.shape[0]
    gw = half // GMLP_GROUPS
    assert w_in.shape == (d, 2 * half) and w_s.shape == (GMLP_GROUPS, CHUNK, CHUNK)
    bs_full = jnp.broadcast_to(b_s[:, :, None], (GMLP_GROUPS, CHUNK, gw))
    row = pl.BlockSpec((tm, d), lambda i: (i, 0))
    return pl.pallas_call(
        _odd_kernel,
        grid=(t // tm,),
        in_specs=[row, _resident((1, d)), _resident((d, 2 * half)), _resident((1, half)),
                  _resident((1, half)), _resident((GMLP_GROUPS, CHUNK, CHUNK)),
                  _resident((GMLP_GROUPS, CHUNK, gw)), _resident((half, d))],
        out_specs=row,
        out_shape=jax.ShapeDtypeStruct((t, d), F32),
        compiler_params=_params("parallel"),
        name="odd_mixer",
    )(x2, gain.reshape(1, d), w_in.astype(BF16), ln_g.reshape(1, half), ln_b.reshape(1, half),
      w_s.astype(BF16), bs_full, (0.5 * w_out).astype(BF16))


def kernel(x, ln_ffn1, ffn1_w_gate, ffn1_w_up, ffn1_w_down, ln_mix, even_w_in, ret_decay_fwd, ret_decay_bwd, att_q_norm, att_k_norm, even_w_out, odd_w_in, sgu_ln_g, sgu_ln_b, sgu_w_s, sgu_b_s, odd_w_out, ln_ffn2, ffn2_w_gate, ffn2_w_up, ffn2_w_down, final_norm):
    b, s, d = x.shape
    depth = ln_ffn1.shape[0]
    t = b * s
    x2 = x.reshape(t, d)
    for l in range(depth):
        x2 = _ffn(x2, ln_ffn1[l], ffn1_w_gate, ffn1_w_up, ffn1_w_down, l)
        mixer_out = None
        if l % 2 == 0:
            e = l // 2
            rq, rk, rv, rg, aq, ak, avt = _even_in(x2.reshape(b, s, d), ln_mix[l], even_w_in[e],
                                                   att_q_norm[e], att_k_norm[e])
            ret = _retention(rq, rk, rv, rg, ret_decay_fwd[e], ret_decay_bwd[e])
            att = _attention(aq, ak, avt)
            mixer_out = (ret.reshape(t, -1), att.reshape(t, -1), even_w_out[e])
        else:
            o = l // 2
            x2 = _odd(x2, ln_mix[l], odd_w_in[o], sgu_ln_g[o], sgu_ln_b[o], sgu_w_s[o],
                      sgu_b_s[o], odd_w_out[o])
        x2 = _ffn(x2, ln_ffn2[l], ffn2_w_gate, ffn2_w_up, ffn2_w_down, l, mixer_out=mixer_out,
                  final_gain=final_norm if l == depth - 1 else None)
    return x2.reshape(b, s, d)
```

```python
import functools

import numpy as np
import jax
import jax.numpy as jnp
from jax import lax
from jax.experimental import pallas as pl
from jax.experimental.pallas import tpu as pltpu

F32 = jnp.float32
BF16 = jnp.bfloat16

EPS = 1e-6
GRID_W = 64
CHUNK = 128
ROPE_THETA = 10000.0
RET_HEADS = 4
RET_DIM = 128
RET_HEADS_PER_STEP = 4
ATT_Q_HEADS = 8
ATT_KV_HEADS = 2
ATT_DIM = 64
ATT_GROUP = ATT_Q_HEADS // ATT_KV_HEADS
ATT_Q_SCALE = ATT_DIM ** -0.5 * float(np.log2(np.e))
GMLP_GROUPS = 8

LANES = 128
BF16_SUBLANES = 16
VMEM_LIMIT_BYTES = 56 * 1024 * 1024

FFN_TM = 1024
STAGE_ROWS = 256
FFN_SUB = 256
EVEN_TM = 1024
EVEN_SUB = 256
ODD_TM = 1024
ODD_SUB = 256
ATT_TQ = 512
ATT_UNIT_Q = 256
ATT_KEY_CHUNK = 256
MXU_N = 256


def _params(*sem):
    return pltpu.CompilerParams(dimension_semantics=sem, vmem_limit_bytes=VMEM_LIMIT_BYTES)


def _resident(shape):
    nd = len(shape)
    return pl.BlockSpec(shape, lambda *_: (0,) * nd, pipeline_mode=pl.Buffered(1))


def _rms(x):
    return x * lax.rsqrt(jnp.mean(x * x, axis=-1, keepdims=True) + EPS)


def _sigmoid(x):
    return 1.0 / (1.0 + jnp.exp(-x))


def _dot(a, b):
    return jnp.dot(a, b, preferred_element_type=F32)


def _col_chunks(n, step):
    return tuple((c, min(c + step, n)) for c in range(0, n, step))


def _stage_weights(pairs, stage_ref, sem):
    jobs = [(src, dst, r0) for src, dst in pairs for r0 in range(0, dst.shape[0], STAGE_ROWS)]

    def copy(j):
        src, dst, r0 = jobs[j]
        slot = j % 2
        return pltpu.make_async_copy(src.at[pl.ds(r0, STAGE_ROWS), :],
                                     stage_ref.at[slot, :, pl.ds(0, dst.shape[1])], sem.at[slot])

    copy(0).start()
    for j, (src, dst, r0) in enumerate(jobs):
        if j + 1 < len(jobs):
            copy(j + 1).start()
        copy(j).wait()
        dst[pl.ds(r0, STAGE_ROWS), :] = stage_ref[j % 2, :, :dst.shape[1]].astype(BF16)


def _ffn_kernel(*refs, layer, ff_chunks, mixer_out, final):
    wg_ref, wu_ref, wd_ref, stage_ref, sem = refs[-5:]
    x_ref, g_ref, wg_hbm, wu_hbm, wd_hbm = refs[:5]
    extra = list(refs[5:-6])
    o_ref = refs[-6]
    if mixer_out:
        r_ref, a_ref, wm_ref = extra[:3]
        extra = extra[3:]
        rw = r_ref.shape[1]
    if final:
        fg_ref, = extra

    @pl.when(pl.program_id(0) == 0)
    def _():
        _stage_weights([(wg_hbm.at[layer], wg_ref), (wu_hbm.at[layer], wu_ref),
                        (wd_hbm.at[layer], wd_ref)], stage_ref, sem)

    tm = x_ref.shape[0]
    sub = min(FFN_SUB, tm)
    if mixer_out:
        xm = x_ref[...] + _dot(r_ref[...], wm_ref[:rw, :]) + _dot(a_ref[...], wm_ref[rw:, :])
    for r0 in range(0, tm, sub):
        rows = slice(r0, r0 + sub)
        x = xm[rows, :] if mixer_out else x_ref[rows, :]
        hb = (_rms(x) * g_ref[...]).astype(BF16)
        acc = None
        for c0, c1 in ff_chunks:
            gate = _dot(hb, wg_ref[:, c0:c1])
            up = _dot(hb, wu_ref[:, c0:c1])
            a = (gate * _sigmoid(gate) * up).astype(BF16)
            d = _dot(a, wd_ref[c0:c1, :])
            acc = d if acc is None else acc + d
        y = x + 0.5 * acc
        if final:
            y = _rms(y) * fg_ref[...]
        o_ref[rows, :] = y


def _ffn(x2, gain, wg_all, wu_all, wd_all, layer, mixer_out=None, final_gain=None):
    t, d = x2.shape
    ff = wg_all.shape[2]
    tm = min(FFN_TM, t)
    assert d % STAGE_ROWS == 0 and ff % STAGE_ROWS == 0

    def row(w):
        return pl.BlockSpec((tm, w), lambda i: (i, 0))

    hbm = pl.BlockSpec(memory_space=pl.ANY)
    in_specs = [row(d), _resident((1, d)), hbm, hbm, hbm]
    args = [x2, gain.reshape(1, d), wg_all, wu_all, wd_all]
    if mixer_out is not None:
        ret2, att2, w_mix = mixer_out
        in_specs += [row(ret2.shape[1]), row(att2.shape[1]), _resident(w_mix.shape)]
        args += [ret2, att2, w_mix.astype(BF16)]
    if final_gain is not None:
        in_specs.append(_resident((1, d)))
        args.append(final_gain.reshape(1, d))
    name = "ffn" + ("_mix" if mixer_out is not None else "") + ("_final" if final_gain is not None else "")
    return pl.pallas_call(
        functools.partial(_ffn_kernel, layer=layer, ff_chunks=_col_chunks(ff, 4 * MXU_N),
                          mixer_out=mixer_out is not None, final=final_gain is not None),
        grid=(t // tm,),
        in_specs=in_specs,
        out_specs=row(d),
        out_shape=jax.ShapeDtypeStruct((t, d), F32),
        scratch_shapes=[pltpu.VMEM((d, ff), BF16), pltpu.VMEM((d, ff), BF16), pltpu.VMEM((ff, d), BF16),
                        pltpu.VMEM((2, STAGE_ROWS, max(d, ff)), F32), pltpu.SemaphoreType.DMA((2,))],
        compiler_params=_params("arbitrary"),
        name=name,
    )(*args)


def _rope(xb, cos, sin_signed, even_lane):
    nxt = pltpu.roll(xb, LANES - 1, 1)
    prv = pltpu.roll(xb, 1, 1)
    return xb * cos + jnp.where(even_lane, nxt, prv) * sin_signed


def _group_mean_sq(x, e_ref):
    sq = x * x
    hi = sq.astype(BF16)
    lo = (sq - hi.astype(F32)).astype(BF16)
    n = x.shape[1]
    w = min(MXU_N, n)
    e = e_ref[:w, :w]
    return jnp.concatenate([_dot(hi[:, c:c + w], e) + _dot(lo[:, c:c + w], e)
                            for c in range(0, n, w)], axis=1)


def _even_in_kernel(x_ref, g_ref, w_ref, e_ref, cr_ref, sr_ref, ca_ref, sa_ref, gq_ref, gk_ref,
                    rq_ref, rk_ref, rv_ref, rg_ref, aq_ref, ak_ref, avt_ref, *, splits):
    o_rq, o_rk, o_rv, o_rg, o_aq, o_ak, o_av, o_end = splits
    tm = x_ref.shape[0]
    sub = min(EVEN_SUB, tm)
    even_lane = (lax.broadcasted_iota(jnp.int32, (sub, LANES), 1) % 2) == 0
    for r0 in range(0, tm, sub):
        rows = slice(r0, r0 + sub)
        hb = (_rms(x_ref[rows, :]) * g_ref[...]).astype(BF16)
        cr, sr = cr_ref[rows, :], sr_ref[rows, :]
        ca, sa = ca_ref[rows, :], sa_ref[rows, :]

        def proj(c0, c1):
            return _dot(hb, w_ref[:, c0:c1])

        aq = proj(o_aq, o_ak)
        ak = proj(o_ak, o_av)
        rq = proj(o_rq, o_rk)
        rk = proj(o_rk, o_rv)
        aq = aq * lax.rsqrt(_group_mean_sq(aq, e_ref) + EPS) * gq_ref[...]
        for j in range(aq.shape[1] // LANES):
            sl = slice(j * LANES, (j + 1) * LANES)
            aq_ref[rows, sl] = (_rope(aq[:, sl], ca, sa, even_lane) * ATT_Q_SCALE).astype(BF16)
        ak = ak * lax.rsqrt(_group_mean_sq(ak, e_ref) + EPS) * gk_ref[...]
        ak = _rope(ak, ca, sa, even_lane).astype(BF16)
        avt = proj(o_av, o_end).T.astype(BF16)
        for h in range(ATT_KV_HEADS):
            sl = slice(h * ATT_DIM, (h + 1) * ATT_DIM)
            ak_ref[h, rows, :] = ak[:, sl]
            avt_ref[h, :, rows] = avt[sl, :]

        for h in range(RET_HEADS):
            sl = slice(h * RET_DIM, (h + 1) * RET_DIM)
            rq_ref[rows, sl] = _rope(rq[:, sl], cr, sr, even_lane).astype(BF16)
            rk_ref[rows, sl] = (_rope(rk[:, sl], cr, sr, even_lane) * RET_DIM ** -0.5).astype(BF16)
        rv_ref[rows, :] = proj(o_rv, o_rg).astype(BF16)
        rg_ref[rows, :] = proj(o_rg, o_aq)


def _rope_tables(seq, head_dim):
    rows = seq // GRID_W
    row = jnp.repeat(jnp.arange(rows), GRID_W).astype(F32)
    col = jnp.tile(jnp.arange(GRID_W), rows).astype(F32)
    axis_dim = head_dim // 2
    freqs = ROPE_THETA ** (-jnp.arange(0, axis_dim, 2, dtype=F32) / axis_dim)
    ang = jnp.concatenate([row[:, None] * freqs[None, :], col[:, None] * freqs[None, :]], axis=-1)
    cos = jnp.repeat(jnp.cos(ang), 2, axis=-1)
    sign = jnp.tile(jnp.array([-1.0, 1.0], F32), head_dim // 2)
    sin = jnp.repeat(jnp.sin(ang), 2, axis=-1) * sign[None, :]
    reps = LANES // head_dim
    return jnp.tile(cos, (1, reps)), jnp.tile(sin, (1, reps))


def _even_in(x3, gain, w_in, gq, gk):
    b, s, d = x3.shape
    tm = min(EVEN_TM, s)
    ret_w = RET_HEADS * RET_DIM
    att_w = ATT_Q_HEADS * ATT_DIM
    kv_w = ATT_KV_HEADS * ATT_DIM
    sizes = (ret_w, ret_w, ret_w, ret_w, att_w, kv_w, kv_w)
    splits = (0,) + tuple(int(v) for v in np.cumsum(sizes))
    n_in = splits[-1]
    assert w_in.shape == (d, n_in)
    cr, sr = _rope_tables(s, RET_DIM)
    ca, sa = _rope_tables(s, ATT_DIM)
    grp = np.arange(MXU_N) // ATT_DIM
    e = jnp.asarray((grp[:, None] == grp[None, :]).astype(np.float32) / ATT_DIM, BF16)
    gq_t = jnp.tile(gq, ATT_Q_HEADS).reshape(1, att_w)
    gk_t = jnp.tile(gk, ATT_KV_HEADS).reshape(1, kv_w)

    def row(w):
        return pl.BlockSpec((None, tm, w), lambda bi, si: (bi, si, 0))

    tab = pl.BlockSpec((tm, LANES), lambda bi, si: (si, 0))
    k_out = pl.BlockSpec((None, ATT_KV_HEADS, tm, ATT_DIM), lambda bi, si: (bi, 0, si, 0))
    vt_out = pl.BlockSpec((None, ATT_KV_HEADS, ATT_DIM, tm), lambda bi, si: (bi, 0, 0, si))
    return pl.pallas_call(
        functools.partial(_even_in_kernel, splits=splits),
        grid=(b, s // tm),
        in_specs=[row(d), _resident((1, d)), _resident((d, n_in)), _resident((MXU_N, MXU_N)),
                  tab, tab, tab, tab, _resident((1, att_w)), _resident((1, kv_w))],
        out_specs=[row(ret_w), row(ret_w), row(ret_w), row(ret_w), row(att_w), k_out, vt_out],
        out_shape=[jax.ShapeDtypeStruct((b, s, ret_w), BF16),
                   jax.ShapeDtypeStruct((b, s, ret_w), BF16),
                   jax.ShapeDtypeStruct((b, s, ret_w), BF16),
                   jax.ShapeDtypeStruct((b, s, ret_w), F32),
                   jax.ShapeDtypeStruct((b, s, att_w), BF16),
                   jax.ShapeDtypeStruct((b, ATT_KV_HEADS, s, ATT_DIM), BF16),
                   jax.ShapeDtypeStruct((b, ATT_KV_HEADS, ATT_DIM, s), BF16)],
        compiler_params=_params("parallel", "parallel"),
        name="even_in",
    )(x3, gain.reshape(1, d), w_in.astype(BF16), e, cr, sr, ca, sa, gq_t, gk_t)


def _ret_kernel(df_ref, db_ref, q_ref, k_ref, v_ref, g_ref, o_ref, st_ref):
    c = CHUNK
    d = RET_DIM
    n_chunks = q_ref.shape[0] // c
    heads = range(q_ref.shape[1] // d)
    ri = lax.broadcasted_iota(jnp.int32, (c, c), 0).astype(F32)
    ci = lax.broadcasted_iota(jnp.int32, (c, c), 1).astype(F32)
    diff = ri - ci
    tdot = functools.partial(lax.dot_general, dimension_numbers=(((0,), (0,)), ((), ())),
                             preferred_element_type=F32)
    mask, kw_f, qw_f, kw_b, qw_b, dec_f, dec_b = [], [], [], [], [], [], []
    for h in heads:
        lam_f = jnp.exp(df_ref[h])
        lam_b = jnp.exp(db_ref[h])
        mask.append(jnp.where(diff >= 0,
                              jnp.exp(-lam_f * jnp.maximum(diff, 0.0)),
                              jnp.exp(-lam_b * jnp.maximum(-diff, 0.0))))
        kw_f.append(jnp.exp(-lam_f * (c - 1.0 - ri)))
        qw_f.append(jnp.exp(-lam_f * (ri + 1.0)))
        kw_b.append(jnp.exp(-lam_b * ri))
        qw_b.append(jnp.exp(-lam_b * (c - ri)))
        dec_f.append(jnp.exp(-lam_f * c))
        dec_b.append(jnp.exp(-lam_b * c))

    def blk(ref, n, h):
        return ref[n * c:(n + 1) * c, h * d:(h + 1) * d]

    st = [jnp.zeros((c, c), F32) for _ in heads]
    for n in range(n_chunks):
        for h in heads:
            st_ref[h, n, :c, :] = st[h].astype(BF16)
            k = blk(k_ref, n, h).astype(F32)
            st[h] = dec_f[h] * st[h] + tdot((k * kw_f[h]).astype(BF16), blk(v_ref, n, h))
    st = [jnp.zeros((c, c), F32) for _ in heads]
    for n in reversed(range(n_chunks)):
        for h in heads:
            st_ref[h, n, c:, :] = st[h].astype(BF16)
            k = blk(k_ref, n, h).astype(F32)
            st[h] = dec_b[h] * st[h] + tdot((k * kw_b[h]).astype(BF16), blk(v_ref, n, h))

    for n in range(n_chunks):
        for h in heads:
            qb = blk(q_ref, n, h)
            q = qb.astype(F32)
            s = lax.dot_general(qb, blk(k_ref, n, h), (((1,), (1,)), ((), ())),
                                preferred_element_type=F32) * mask[h]
            qq = jnp.concatenate([(q * qw_f[h]).astype(BF16), (q * qw_b[h]).astype(BF16)], axis=1)
            o = _dot(s.astype(BF16), blk(v_ref, n, h)) + _dot(qq, st_ref[h, n])
            gate = blk(g_ref, n, h)
            o_ref[n * c:(n + 1) * c, h * d:(h + 1) * d] = (
                _rms(o) * (gate * _sigmoid(gate))).astype(BF16)


def _retention(rq, rk, rv, rg, decay_f, decay_b):
    b, s, w = rq.shape
    heads = w // RET_DIM
    hp = RET_HEADS_PER_STEP
    n_chunks = s // CHUNK
    blk = pl.BlockSpec((None, s, hp * RET_DIM), lambda bi, hi: (bi, 0, hi))
    dec = pl.BlockSpec((hp, 1, LANES), lambda bi, hi: (hi, 0, 0))
    df = jnp.broadcast_to(decay_f.reshape(heads, 1, 1), (heads, 1, LANES))
    db = jnp.broadcast_to(decay_b.reshape(heads, 1, 1), (heads, 1, LANES))
    return pl.pallas_call(
        _ret_kernel,
        grid=(b, heads // hp),
        in_specs=[dec, dec, blk, blk, blk, blk],
        out_specs=blk,
        out_shape=jax.ShapeDtypeStruct((b, s, w), BF16),
        scratch_shapes=[pltpu.VMEM((hp, n_chunks, 2 * CHUNK, RET_DIM), BF16)],
        compiler_params=_params("parallel", "parallel"),
        name="retention",
    )(df, db, rq, rk, rv, rg)


def _att_kernel(q_ref, k_ref, vt_ref, o_ref, s0_ref, s1_ref, m0_ref, m1_ref):
    i = pl.program_id(0)
    d = k_ref.shape[1]
    n_units, _, uq = s0_ref.shape
    units = [(r, h) for r in range(n_units // ATT_GROUP) for h in range(ATT_GROUP)]

    @pl.when(i == 0)
    def _():
        s1_ref[...] = jnp.zeros(s1_ref.shape, F32)
        m1_ref[...] = jnp.zeros(m1_ref.shape, F32)

    def step(sw_ref, mw_ref, sr_ref, mr_ref):
        n_keys = k_ref.shape[0]
        kc = min(ATT_KEY_CHUNK, n_keys)
        ones = jnp.ones((BF16_SUBLANES, kc), BF16)
        outs = []
        for u0 in range(0, n_units, 2):
            pair = (u0, u0 + 1)
            qs = {u: q_ref[units[u][0] * uq:(units[u][0] + 1) * uq,
                           units[u][1] * d:(units[u][1] + 1) * d] for u in pair}
            mx = {u: None for u in pair}
            acc = {u: None for u in pair}
            for c0 in range(0, n_keys, kc):
                rows = slice(c0, c0 + kc)
                for u in pair:
                    st = lax.dot_general(k_ref[rows, :], qs[u], (((1,), (1,)), ((), ())),
                                         preferred_element_type=F32)
                    sw_ref[u, rows, :] = st
                    cm = jnp.max(st, axis=0, keepdims=True)
                    mx[u] = cm if mx[u] is None else jnp.maximum(mx[u], cm)
                vte = jnp.concatenate([vt_ref[:, rows], ones], axis=0)
                for u in pair:
                    p = jnp.exp2(sr_ref[u, rows, :] - mr_ref[u])
                    pv = _dot(vte, p.astype(BF16))
                    acc[u] = pv if acc[u] is None else acc[u] + pv
            for u in pair:
                mw_ref[u] = mx[u]
                outs.append(acc[u][:d, :] * (1.0 / acc[u][d:d + 1, :]))
            if len(outs) == ATT_GROUP:
                r = units[u0][0]
                o_ref[r * uq:(r + 1) * uq, :] = jnp.concatenate(outs, axis=0).T.astype(BF16)
                outs = []

    @pl.when(i % 2 == 0)
    def _():
        step(s0_ref, m0_ref, s1_ref, m1_ref)

    @pl.when(i % 2 == 1)
    def _():
        step(s1_ref, m1_ref, s0_ref, m0_ref)


def _attention(aq, ak, avt):
    b, s, w = aq.shape
    tq = min(ATT_TQ, s)
    uq = min(ATT_UNIT_Q, tq)
    gw = ATT_GROUP * ATT_DIM
    nq = s // tq
    n_tiles = b * ATT_KV_HEADS * nq
    n_units = ATT_GROUP * (tq // uq)

    def tile(t):
        return t // (ATT_KV_HEADS * nq), (t // nq) % ATT_KV_HEADS, t % nq

    def score_tile(i):
        return tile(jnp.minimum(i, n_tiles - 1))

    def softmax_tile(i):
        return tile(jnp.maximum(i - 1, 0))

    def qmap(bgq):
        bi, gi, qi = bgq
        return bi, qi, gi

    def kvmap(bgq):
        bi, gi, _ = bgq
        return bi, gi, 0, 0

    return pl.pallas_call(
        _att_kernel,
        grid=(n_tiles + 1,),
        in_specs=[pl.BlockSpec((None, tq, gw), lambda i: qmap(score_tile(i))),
                  pl.BlockSpec((None, None, s, ATT_DIM), lambda i: kvmap(score_tile(i))),
                  pl.BlockSpec((None, None, ATT_DIM, s), lambda i: kvmap(softmax_tile(i)))],
        out_specs=pl.BlockSpec((None, tq, gw), lambda i: qmap(softmax_tile(i))),
        out_shape=jax.ShapeDtypeStruct((b, s, w), BF16),
        scratch_shapes=[pltpu.VMEM((n_units, s, uq), F32), pltpu.VMEM((n_units, s, uq), F32),
                        pltpu.VMEM((n_units, 1, uq), F32), pltpu.VMEM((n_units, 1, uq), F32)],
        compiler_params=_params("arbitrary"),
        name="attention",
    )(aq, ak, avt)


def _gelu2(x):
    return x + x * lax.erf(x * (2.0 ** -0.5))


def _spatial_mix(w, bias, vg):
    n = vg.shape[0] // CHUNK
    gw = vg.shape[1]
    wide = _dot(w, jnp.concatenate([vg[i * CHUNK:(i + 1) * CHUNK, :] for i in range(n)], axis=1))
    return jnp.concatenate([wide[:, i * gw:(i + 1) * gw] + bias for i in range(n)], axis=0)


def _odd_kernel(x_ref, g_ref, wi_ref, lg_ref, lb_ref, ws_ref, bs_ref, wo_ref, o_ref):
    half = wo_ref.shape[0]
    gw = half // GMLP_GROUPS
    pair = 2 * gw
    tm = x_ref.shape[0]
    sub = min(ODD_SUB, tm)
    blocks = [slice(r0, r0 + sub) for r0 in range(0, tm, sub)]

    hbs, vns = [], []
    for rows in blocks:
        hb = (_rms(x_ref[rows, :]) * g_ref[...]).astype(BF16)
        v2 = _gelu2(_dot(hb, wi_ref[:, half:]))
        mu = jnp.mean(v2, axis=-1, keepdims=True)
        vc = v2 - mu
        var = jnp.mean(vc * vc, axis=-1, keepdims=True)
        hbs.append(hb)
        vns.append((vc * lax.rsqrt(var + 4.0 * EPS) * lg_ref[...] + lb_ref[...]).astype(BF16))

    for rows, hb, vn in zip(blocks, hbs, vns):
        gated = []
        for c0 in range(0, half, pair):
            u = _gelu2(_dot(hb, wi_ref[:, c0:c0 + pair]))
            mixed = jnp.concatenate([_spatial_mix(ws_ref[g], bs_ref[g], vn[:, g * gw:(g + 1) * gw])
                                     for g in (c0 // gw, c0 // gw + 1)], axis=1)
            gated.append((u * mixed).astype(BF16))
        o_ref[rows, :] = x_ref[rows, :] + _dot(jnp.concatenate(gated, axis=1), wo_ref[...])


def _odd(x2, gain, w_in, ln_g, ln_b, w_s, b_s, w_out):
    t, d = x2.shape
    tm = min(ODD_TM, t)
    half = w_out.shape[0]
    gw = half // GMLP_GROUPS
    assert w_in.shape == (d, 2 * half) and w_s.shape == (GMLP_GROUPS, CHUNK, CHUNK)
    bs_full = jnp.broadcast_to(b_s[:, :, None], (GMLP_GROUPS, CHUNK, gw))
    row = pl.BlockSpec((tm, d), lambda i: (i, 0))
    return pl.pallas_call(
        _odd_kernel,
        grid=(t // tm,),
        in_specs=[row, _resident((1, d)), _resident((d, 2 * half)), _resident((1, half)),
                  _resident((1, half)), _resident((GMLP_GROUPS, CHUNK, CHUNK)),
                  _resident((GMLP_GROUPS, CHUNK, gw)), _resident((half, d))],
        out_specs=row,
        out_shape=jax.ShapeDtypeStruct((t, d), F32),
        compiler_params=_params("parallel"),
        name="odd_mixer",
    )(x2, gain.reshape(1, d), w_in.astype(BF16), ln_g.reshape(1, half), ln_b.reshape(1, half),
      w_s.astype(BF16), bs_full, (0.5 * w_out).astype(BF16))


def kernel(x, ln_ffn1, ffn1_w_gate, ffn1_w_up, ffn1_w_down, ln_mix, even_w_in, ret_decay_fwd, ret_decay_bwd, att_q_norm, att_k_norm, even_w_out, odd_w_in, sgu_ln_g, sgu_ln_b, sgu_w_s, sgu_b_s, odd_w_out, ln_ffn2, ffn2_w_gate, ffn2_w_up, ffn2_w_down, final_norm):
    b, s, d = x.shape
    depth = ln_ffn1.shape[0]
    t = b * s
    x2 = x.reshape(t, d)
    for l in range(depth):
        x2 = _ffn(x2, ln_ffn1[l], ffn1_w_gate, ffn1_w_up, ffn1_w_down, l)
        mixer_out = None
        if l % 2 == 0:
            e = l // 2
            rq, rk, rv, rg, aq, ak, avt = _even_in(x2.reshape(b, s, d), ln_mix[l], even_w_in[e],
                                                   att_q_norm[e], att_k_norm[e])
            ret = _retention(rq, rk, rv, rg, ret_decay_fwd[e], ret_decay_bwd[e])
            att = _attention(aq, ak, avt)
            mixer_out = (ret.reshape(t, -1), att.reshape(t, -1), even_w_out[e])
        else:
            o = l // 2
            x2 = _odd(x2, ln_mix[l], odd_w_in[o], sgu_ln_g[o], sgu_ln_b[o], sgu_w_s[o],
                      sgu_b_s[o], odd_w_out[o])
        x2 = _ffn(x2, ln_ffn2[l], ffn2_w_gate, ffn2_w_up, ffn2_w_down, l, mixer_out=mixer_out,
                  final_gain=final_norm if l == depth - 1 else None)
    return x2.reshape(b, s, d)
```

```python
import functools

import numpy as np
import jax
import jax.numpy as jnp
from jax import lax
from jax.experimental import pallas as pl
from jax.experimental.pallas import tpu as pltpu

F32 = jnp.float32
BF16 = jnp.bfloat16

EPS = 1e-6
GRID_W = 64
CHUNK = 128
ROPE_THETA = 10000.0
RET_HEADS = 4
RET_DIM = 128
RET_HEADS_PER_STEP = 4
ATT_Q_HEADS = 8
ATT_KV_HEADS = 2
ATT_DIM = 64
ATT_GROUP = ATT_Q_HEADS // ATT_KV_HEADS
ATT_Q_SCALE = ATT_DIM ** -0.5 * float(np.log2(np.e))
GMLP_GROUPS = 8

LANES = 128
BF16_SUBLANES = 16
VMEM_LIMIT_BYTES = 56 * 1024 * 1024

FFN_TM = 1024
STAGE_ROWS = 256
FFN_SUB = 256
EVEN_TM = 1024
EVEN_SUB = 256
ODD_TM = 1024
ODD_SUB = 256
ATT_TQ = 1024
ATT_UNIT_Q = 256
ATT_KEY_CHUNK = 256
MXU_N = 256


def _params(*sem):
    return pltpu.CompilerParams(dimension_semantics=sem, vmem_limit_bytes=VMEM_LIMIT_BYTES)


def _resident(shape):
    nd = len(shape)
    return pl.BlockSpec(shape, lambda *_: (0,) * nd, pipeline_mode=pl.Buffered(1))


def _rms(x):
    return x * lax.rsqrt(jnp.mean(x * x, axis=-1, keepdims=True) + EPS)


def _sigmoid(x):
    return 1.0 / (1.0 + jnp.exp(-x))


def _dot(a, b):
    return jnp.dot(a, b, preferred_element_type=F32)


def _col_chunks(n, step):
    return tuple((c, min(c + step, n)) for c in range(0, n, step))


def _stage_weights(pairs, stage_ref, sem):
    jobs = [(src, dst, r0) for src, dst in pairs for r0 in range(0, dst.shape[0], STAGE_ROWS)]

    def copy(j):
        src, dst, r0 = jobs[j]
        slot = j % 2
        return pltpu.make_async_copy(src.at[pl.ds(r0, STAGE_ROWS), :],
                                     stage_ref.at[slot, :, pl.ds(0, dst.shape[1])], sem.at[slot])

    copy(0).start()
    for j, (src, dst, r0) in enumerate(jobs):
        if j + 1 < len(jobs):
            copy(j + 1).start()
        copy(j).wait()
        dst[pl.ds(r0, STAGE_ROWS), :] = stage_ref[j % 2, :, :dst.shape[1]].astype(BF16)


def _ffn_kernel(*refs, layer, ff_chunks, mixer_out, final):
    wg_ref, wu_ref, wd_ref, stage_ref, sem = refs[-5:]
    x_ref, g_ref, wg_hbm, wu_hbm, wd_hbm = refs[:5]
    extra = list(refs[5:-6])
    o_ref = refs[-6]
    if mixer_out:
        r_ref, a_ref, wm_ref = extra[:3]
        extra = extra[3:]
        rw = r_ref.shape[1]
    if final:
        fg_ref, = extra

    @pl.when(pl.program_id(0) == 0)
    def _():
        _stage_weights([(wg_hbm.at[layer], wg_ref), (wu_hbm.at[layer], wu_ref),
                        (wd_hbm.at[layer], wd_ref)], stage_ref, sem)

    tm = x_ref.shape[0]
    sub = min(FFN_SUB, tm)
    if mixer_out:
        xm = x_ref[...] + _dot(r_ref[...], wm_ref[:rw, :]) + _dot(a_ref[...], wm_ref[rw:, :])
    for r0 in range(0, tm, sub):
        rows = slice(r0, r0 + sub)
        x = xm[rows, :] if mixer_out else x_ref[rows, :]
        hb = (_rms(x) * g_ref[...]).astype(BF16)
        acc = None
        for c0, c1 in ff_chunks:
            gate = _dot(hb, wg_ref[:, c0:c1])
            up = _dot(hb, wu_ref[:, c0:c1])
            a = (gate * _sigmoid(gate) * up).astype(BF16)
            d = _dot(a, wd_ref[c0:c1, :])
            acc = d if acc is None else acc + d
        y = x + 0.5 * acc
        if final:
            y = _rms(y) * fg_ref[...]
        o_ref[rows, :] = y


def _ffn(x2, gain, wg_all, wu_all, wd_all, layer, mixer_out=None, final_gain=None):
    t, d = x2.shape
    ff = wg_all.shape[2]
    tm = min(FFN_TM, t)
    assert d % STAGE_ROWS == 0 and ff % STAGE_ROWS == 0

    def row(w):
        return pl.BlockSpec((tm, w), lambda i: (i, 0))

    hbm = pl.BlockSpec(memory_space=pl.ANY)
    in_specs = [row(d), _resident((1, d)), hbm, hbm, hbm]
    args = [x2, gain.reshape(1, d), wg_all, wu_all, wd_all]
    if mixer_out is not None:
        ret2, att2, w_mix = mixer_out
        in_specs += [row(ret2.shape[1]), row(att2.shape[1]), _resident(w_mix.shape)]
        args += [ret2, att2, w_mix.astype(BF16)]
    if final_gain is not None:
        in_specs.append(_resident((1, d)))
        args.append(final_gain.reshape(1, d))
    name = "ffn" + ("_mix" if mixer_out is not None else "") + ("_final" if final_gain is not None else "")
    return pl.pallas_call(
        functools.partial(_ffn_kernel, layer=layer, ff_chunks=_col_chunks(ff, 4 * MXU_N),
                          mixer_out=mixer_out is not None, final=final_gain is not None),
        grid=(t // tm,),
        in_specs=in_specs,
        out_specs=row(d),
        out_shape=jax.ShapeDtypeStruct((t, d), F32),
        scratch_shapes=[pltpu.VMEM((d, ff), BF16), pltpu.VMEM((d, ff), BF16), pltpu.VMEM((ff, d), BF16),
                        pltpu.VMEM((2, STAGE_ROWS, max(d, ff)), F32), pltpu.SemaphoreType.DMA((2,))],
        compiler_params=_params("arbitrary"),
        name=name,
    )(*args)


def _rope(xb, cos, sin_signed, even_lane):
    nxt = pltpu.roll(xb, LANES - 1, 1)
    prv = pltpu.roll(xb, 1, 1)
    return xb * cos + jnp.where(even_lane, nxt, prv) * sin_signed


def _group_mean_sq(x, e_ref):
    sq = x * x
    hi = sq.astype(BF16)
    lo = (sq - hi.astype(F32)).astype(BF16)
    n = x.shape[1]
    w = min(MXU_N, n)
    e = e_ref[:w, :w]
    return jnp.concatenate([_dot(hi[:, c:c + w], e) + _dot(lo[:, c:c + w], e)
                            for c in range(0, n, w)], axis=1)


def _even_in_kernel(x_ref, g_ref, w_ref, e_ref, cr_ref, sr_ref, ca_ref, sa_ref, gq_ref, gk_ref,
                    rq_ref, rk_ref, rv_ref, rg_ref, aq_ref, ak_ref, avt_ref, *, splits):
    o_rq, o_rk, o_rv, o_rg, o_aq, o_ak, o_av, o_end = splits
    tm = x_ref.shape[0]
    sub = min(EVEN_SUB, tm)
    even_lane = (lax.broadcasted_iota(jnp.int32, (sub, LANES), 1) % 2) == 0
    for r0 in range(0, tm, sub):
        rows = slice(r0, r0 + sub)
        hb = (_rms(x_ref[rows, :]) * g_ref[...]).astype(BF16)
        cr, sr = cr_ref[rows, :], sr_ref[rows, :]
        ca, sa = ca_ref[rows, :], sa_ref[rows, :]

        def proj(c0, c1):
            return _dot(hb, w_ref[:, c0:c1])

        aq = proj(o_aq, o_ak)
        ak = proj(o_ak, o_av)
        rq = proj(o_rq, o_rk)
        rk = proj(o_rk, o_rv)
        aq = aq * lax.rsqrt(_group_mean_sq(aq, e_ref) + EPS) * gq_ref[...]
        for j in range(aq.shape[1] // LANES):
            sl = slice(j * LANES, (j + 1) * LANES)
            aq_ref[rows, sl] = (_rope(aq[:, sl], ca, sa, even_lane) * ATT_Q_SCALE).astype(BF16)
        ak = ak * lax.rsqrt(_group_mean_sq(ak, e_ref) + EPS) * gk_ref[...]
        ak = _rope(ak, ca, sa, even_lane).astype(BF16)
        avt = proj(o_av, o_end).T.astype(BF16)
        for h in range(ATT_KV_HEADS):
            sl = slice(h * ATT_DIM, (h + 1) * ATT_DIM)
            ak_ref[h, rows, :] = ak[:, sl]
            avt_ref[h, :, rows] = avt[sl, :]

        for h in range(RET_HEADS):
            sl = slice(h * RET_DIM, (h + 1) * RET_DIM)
            rq_ref[rows, sl] = _rope(rq[:, sl], cr, sr, even_lane).astype(BF16)
            rk_ref[rows, sl] = (_rope(rk[:, sl], cr, sr, even_lane) * RET_DIM ** -0.5).astype(BF16)
        rv_ref[rows, :] = proj(o_rv, o_rg).astype(BF16)
        rg_ref[rows, :] = proj(o_rg, o_aq)


def _rope_tables(seq, head_dim):
    rows = seq // GRID_W
    row = jnp.repeat(jnp.arange(rows), GRID_W).astype(F32)
    col = jnp.tile(jnp.arange(GRID_W), rows).astype(F32)
    axis_dim = head_dim // 2
    freqs = ROPE_THETA ** (-jnp.arange(0, axis_dim, 2, dtype=F32) / axis_dim)
    ang = jnp.concatenate([row[:, None] * freqs[None, :], col[:, None] * freqs[None, :]], axis=-1)
    cos = jnp.repeat(jnp.cos(ang), 2, axis=-1)
    sign = jnp.tile(jnp.array([-1.0, 1.0], F32), head_dim // 2)
    sin = jnp.repeat(jnp.sin(ang), 2, axis=-1) * sign[None, :]
    reps = LANES // head_dim
    return jnp.tile(cos, (1, reps)), jnp.tile(sin, (1, reps))


def _even_in(x3, gain, w_in, gq, gk):
    b, s, d = x3.shape
    tm = min(EVEN_TM, s)
    ret_w = RET_HEADS * RET_DIM
    att_w = ATT_Q_HEADS * ATT_DIM
    kv_w = ATT_KV_HEADS * ATT_DIM
    sizes = (ret_w, ret_w, ret_w, ret_w, att_w, kv_w, kv_w)
    splits = (0,) + tuple(int(v) for v in np.cumsum(sizes))
    n_in = splits[-1]
    assert w_in.shape == (d, n_in)
    cr, sr = _rope_tables(s, RET_DIM)
    ca, sa = _rope_tables(s, ATT_DIM)
    grp = np.arange(MXU_N) // ATT_DIM
    e = jnp.asarray((grp[:, None] == grp[None, :]).astype(np.float32) / ATT_DIM, BF16)
    gq_t = jnp.tile(gq, ATT_Q_HEADS).reshape(1, att_w)
    gk_t = jnp.tile(gk, ATT_KV_HEADS).reshape(1, kv_w)

    def row(w):
        return pl.BlockSpec((None, tm, w), lambda bi, si: (bi, si, 0))

    tab = pl.BlockSpec((tm, LANES), lambda bi, si: (si, 0))
    k_out = pl.BlockSpec((None, ATT_KV_HEADS, tm, ATT_DIM), lambda bi, si: (bi, 0, si, 0))
    vt_out = pl.BlockSpec((None, ATT_KV_HEADS, ATT_DIM, tm), lambda bi, si: (bi, 0, 0, si))
    return pl.pallas_call(
        functools.partial(_even_in_kernel, splits=splits),
        grid=(b, s // tm),
        in_specs=[row(d), _resident((1, d)), _resident((d, n_in)), _resident((MXU_N, MXU_N)),
                  tab, tab, tab, tab, _resident((1, att_w)), _resident((1, kv_w))],
        out_specs=[row(ret_w), row(ret_w), row(ret_w), row(ret_w), row(att_w), k_out, vt_out],
        out_shape=[jax.ShapeDtypeStruct((b, s, ret_w), BF16),
                   jax.ShapeDtypeStruct((b, s, ret_w), BF16),
                   jax.ShapeDtypeStruct((b, s, ret_w), BF16),
                   jax.ShapeDtypeStruct((b, s, ret_w), F32),
                   jax.ShapeDtypeStruct((b, s, att_w), BF16),
                   jax.ShapeDtypeStruct((b, ATT_KV_HEADS, s, ATT_DIM), BF16),
                   jax.ShapeDtypeStruct((b, ATT_KV_HEADS, ATT_DIM, s), BF16)],
        compiler_params=_params("parallel", "parallel"),
        name="even_in",
    )(x3, gain.reshape(1, d), w_in.astype(BF16), e, cr, sr, ca, sa, gq_t, gk_t)


def _ret_kernel(df_ref, db_ref, q_ref, k_ref, v_ref, g_ref, o_ref, st_ref):
    c = CHUNK
    d = RET_DIM
    n_chunks = q_ref.shape[0] // c
    heads = range(q_ref.shape[1] // d)
    ri = lax.broadcasted_iota(jnp.int32, (c, c), 0).astype(F32)
    ci = lax.broadcasted_iota(jnp.int32, (c, c), 1).astype(F32)
    diff = ri - ci
    tdot = functools.partial(lax.dot_general, dimension_numbers=(((0,), (0,)), ((), ())),
                             preferred_element_type=F32)
    mask, kw_f, qw_f, kw_b, qw_b, dec_f, dec_b = [], [], [], [], [], [], []
    for h in heads:
        lam_f = jnp.exp(df_ref[h])
        lam_b = jnp.exp(db_ref[h])
        mask.append(jnp.where(diff >= 0,
                              jnp.exp(-lam_f * jnp.maximum(diff, 0.0)),
                              jnp.exp(-lam_b * jnp.maximum(-diff, 0.0))))
        kw_f.append(jnp.exp(-lam_f * (c - 1.0 - ri)))
        qw_f.append(jnp.exp(-lam_f * (ri + 1.0)))
        kw_b.append(jnp.exp(-lam_b * ri))
        qw_b.append(jnp.exp(-lam_b * (c - ri)))
        dec_f.append(jnp.exp(-lam_f * c))
        dec_b.append(jnp.exp(-lam_b * c))

    def blk(ref, n, h):
        return ref[n * c:(n + 1) * c, h * d:(h + 1) * d]

    st = [jnp.zeros((c, c), F32) for _ in heads]
    for n in range(n_chunks):
        for h in heads:
            st_ref[h, n, :c, :] = st[h].astype(BF16)
            k = blk(k_ref, n, h).astype(F32)
            st[h] = dec_f[h] * st[h] + tdot((k * kw_f[h]).astype(BF16), blk(v_ref, n, h))
    st = [jnp.zeros((c, c), F32) for _ in heads]
    for n in reversed(range(n_chunks)):
        for h in heads:
            st_ref[h, n, c:, :] = st[h].astype(BF16)
            k = blk(k_ref, n, h).astype(F32)
            st[h] = dec_b[h] * st[h] + tdot((k * kw_b[h]).astype(BF16), blk(v_ref, n, h))

    for n in range(n_chunks):
        for h in heads:
            qb = blk(q_ref, n, h)
            q = qb.astype(F32)
            s = lax.dot_general(qb, blk(k_ref, n, h), (((1,), (1,)), ((), ())),
                                preferred_element_type=F32) * mask[h]
            qq = jnp.concatenate([(q * qw_f[h]).astype(BF16), (q * qw_b[h]).astype(BF16)], axis=1)
            o = _dot(s.astype(BF16), blk(v_ref, n, h)) + _dot(qq, st_ref[h, n])
            gate = blk(g_ref, n, h)
            o_ref[n * c:(n + 1) * c, h * d:(h + 1) * d] = (
                _rms(o) * (gate * _sigmoid(gate))).astype(BF16)


def _retention(rq, rk, rv, rg, decay_f, decay_b):
    b, s, w = rq.shape
    heads = w // RET_DIM
    hp = RET_HEADS_PER_STEP
    n_chunks = s // CHUNK
    blk = pl.BlockSpec((None, s, hp * RET_DIM), lambda bi, hi: (bi, 0, hi))
    dec = pl.BlockSpec((hp, 1, LANES), lambda bi, hi: (hi, 0, 0))
    df = jnp.broadcast_to(decay_f.reshape(heads, 1, 1), (heads, 1, LANES))
    db = jnp.broadcast_to(decay_b.reshape(heads, 1, 1), (heads, 1, LANES))
    return pl.pallas_call(
        _ret_kernel,
        grid=(b, heads // hp),
        in_specs=[dec, dec, blk, blk, blk, blk],
        out_specs=blk,
        out_shape=jax.ShapeDtypeStruct((b, s, w), BF16),
        scratch_shapes=[pltpu.VMEM((hp, n_chunks, 2 * CHUNK, RET_DIM), BF16)],
        compiler_params=_params("parallel", "parallel"),
        name="retention",
    )(df, db, rq, rk, rv, rg)


def _att_kernel(q_ref, k_ref, vt_ref, o_ref, s_ref, m_ref):
    i = pl.program_id(0)
    d = k_ref.shape[1]
    n_units, n_keys, uq = s_ref.shape
    units = [(r, h) for r in range(n_units // ATT_GROUP) for h in range(ATT_GROUP)]
    kc = min(ATT_KEY_CHUNK, n_keys)

    @pl.when(i == 0)
    def _():
        s_ref[...] = jnp.zeros(s_ref.shape, F32)
        m_ref[...] = jnp.zeros(m_ref.shape, F32)

    ones = jnp.ones((BF16_SUBLANES, kc), BF16)
    outs = []
    for u0 in range(0, n_units, 2):
        pair = (u0, u0 + 1)
        qs = {u: q_ref[units[u][0] * uq:(units[u][0] + 1) * uq,
                       units[u][1] * d:(units[u][1] + 1) * d] for u in pair}
        prev_max = {u: m_ref[u] for u in pair}
        mx = {u: None for u in pair}
        acc = {u: None for u in pair}
        for c0 in range(0, n_keys, kc):
            rows = slice(c0, c0 + kc)
            vte = jnp.concatenate([vt_ref[:, rows], ones], axis=0)
            for u in pair:
                p = jnp.exp2(s_ref[u, rows, :] - prev_max[u])
                pv = _dot(vte, p.astype(BF16))
                acc[u] = pv if acc[u] is None else acc[u] + pv
            for u in pair:
                st = lax.dot_general(k_ref[rows, :], qs[u], (((1,), (1,)), ((), ())),
                                     preferred_element_type=F32)
                s_ref[u, rows, :] = st
                cm = jnp.max(st, axis=0, keepdims=True)
                mx[u] = cm if mx[u] is None else jnp.maximum(mx[u], cm)
        for u in pair:
            m_ref[u] = mx[u]
            outs.append(acc[u][:d, :] * (1.0 / acc[u][d:d + 1, :]))
        if len(outs) == ATT_GROUP:
            r = units[u0][0]
            o_ref[r * uq:(r + 1) * uq, :] = jnp.concatenate(outs, axis=0).T.astype(BF16)
            outs = []


def _attention(aq, ak, avt):
    b, s, w = aq.shape
    tq = min(ATT_TQ, s)
    uq = min(ATT_UNIT_Q, tq)
    gw = ATT_GROUP * ATT_DIM
    nq = s // tq
    n_tiles = b * ATT_KV_HEADS * nq
    n_units = ATT_GROUP * (tq // uq)

    def tile(t):
        return t // (ATT_KV_HEADS * nq), (t // nq) % ATT_KV_HEADS, t % nq

    def score_tile(i):
        return tile(jnp.minimum(i, n_tiles - 1))

    def softmax_tile(i):
        return tile(jnp.maximum(i - 1, 0))

    def qmap(bgq):
        bi, gi, qi = bgq
        return bi, qi, gi

    def kvmap(bgq):
        bi, gi, _ = bgq
        return bi, gi, 0, 0

    return pl.pallas_call(
        _att_kernel,
        grid=(n_tiles + 1,),
        in_specs=[pl.BlockSpec((None, tq, gw), lambda i: qmap(score_tile(i))),
                  pl.BlockSpec((None, None, s, ATT_DIM), lambda i: kvmap(score_tile(i))),
                  pl.BlockSpec((None, None, ATT_DIM, s), lambda i: kvmap(softmax_tile(i)))],
        out_specs=pl.BlockSpec((None, tq, gw), lambda i: qmap(softmax_tile(i))),
        out_shape=jax.ShapeDtypeStruct((b, s, w), BF16),
        scratch_shapes=[pltpu.VMEM((n_units, s, uq), F32), pltpu.VMEM((n_units, 1, uq), F32)],
        compiler_params=_params("arbitrary"),
        name="attention",
    )(aq, ak, avt)


def _gelu2(x):
    return x + x * lax.erf(x * (2.0 ** -0.5))


def _spatial_mix(w, bias, vg):
    n = vg.shape[0] // CHUNK
    gw = vg.shape[1]
    wide = _dot(w, jnp.concatenate([vg[i * CHUNK:(i + 1) * CHUNK, :] for i in range(n)], axis=1))
    return jnp.concatenate([wide[:, i * gw:(i + 1) * gw] + bias for i in range(n)], axis=0)


def _odd_kernel(x_ref, g_ref, wi_ref, lg_ref, lb_ref, ws_ref, bs_ref, wo_ref, o_ref):
    half = wo_ref.shape[0]
    gw = half // GMLP_GROUPS
    pair = 2 * gw
    tm = x_ref.shape[0]
    sub = min(ODD_SUB, tm)
    blocks = [slice(r0, r0 + sub) for r0 in range(0, tm, sub)]

    hbs, vns = [], []
    for rows in blocks:
        hb = (_rms(x_ref[rows, :]) * g_ref[...]).astype(BF16)
        v2 = _gelu2(_dot(hb, wi_ref[:, half:]))
        mu = jnp.mean(v2, axis=-1, keepdims=True)
        vc = v2 - mu
        var = jnp.mean(vc * vc, axis=-1, keepdims=True)
        hbs.append(hb)
        vns.append((vc * lax.rsqrt(var + 4.0 * EPS) * lg_ref[...] + lb_ref[...]).astype(BF16))

    for rows, hb, vn in zip(blocks, hbs, vns):
        gated = []
        for c0 in range(0, half, pair):
            u = _gelu2(_dot(hb, wi_ref[:, c0:c0 + pair]))
            mixed = jnp.concatenate([_spatial_mix(ws_ref[g], bs_ref[g], vn[:, g * gw:(g + 1) * gw])
                                     for g in (c0 // gw, c0 // gw + 1)], axis=1)
            gated.append((u * mixed).astype(BF16))
        o_ref[rows, :] = x_ref[rows, :] + _dot(jnp.concatenate(gated, axis=1), wo_ref[...])


def _odd(x2, gain, w_in, ln_g, ln_b, w_s, b_s, w_out):
    t, d = x2.shape
    tm = min(ODD_TM, t)
    half = w_out.shape[0]
    gw = half // GMLP_GROUPS
    assert w_in.shape == (d, 2 * half) and w_s.shape == (GMLP_GROUPS, CHUNK, CHUNK)
    bs_full = jnp.broadcast_to(b_s[:, :, None], (GMLP_GROUPS, CHUNK, gw))
    row = pl.BlockSpec((tm, d), lambda i: (i, 0))
    return pl.pallas_call(
        _odd_kernel,
        grid=(t // tm,),
        in_specs=[row, _resident((1, d)), _resident((d, 2 * half)), _resident((1, half)),
                  _resident((1, half)), _resident((GMLP_GROUPS, CHUNK, CHUNK)),
                  _resident((GMLP_GROUPS, CHUNK, gw)), _resident((half, d))],
        out_specs=row,
        out_shape=jax.ShapeDtypeStruct((t, d), F32),
        compiler_params=_params("parallel"),
        name="odd_mixer",
    )(x2, gain.reshape(1, d), w_in.astype(BF16), ln_g.reshape(1, half), ln_b.reshape(1, half),
      w_s.astype(BF16), bs_full, (0.5 * w_out).astype(BF16))


def kernel(x, ln_ffn1, ffn1_w_gate, ffn1_w_up, ffn1_w_down, ln_mix, even_w_in, ret_decay_fwd, ret_decay_bwd, att_q_norm, att_k_norm, even_w_out, odd_w_in, sgu_ln_g, sgu_ln_b, sgu_w_s, sgu_b_s, odd_w_out, ln_ffn2, ffn2_w_gate, ffn2_w_up, ffn2_w_down, final_norm):
    b, s, d = x.shape
    depth = ln_ffn1.shape[0]
    t = b * s
    x2 = x.reshape(t, d)
    for l in range(depth):
        x2 = _ffn(x2, ln_ffn1[l], ffn1_w_gate, ffn1_w_up, ffn1_w_down, l)
        mixer_out = None
        if l % 2 == 0:
            e = l // 2
            rq, rk, rv, rg, aq, ak, avt = _even_in(x2.reshape(b, s, d), ln_mix[l], even_w_in[e],
                                                   att_q_norm[e], att_k_norm[e])
            ret = _retention(rq, rk, rv, rg, ret_decay_fwd[e], ret_decay_bwd[e])
            att = _attention(aq, ak, avt)
            mixer_out = (ret.reshape(t, -1), att.reshape(t, -1), even_w_out[e])
        else:
            o = l // 2
            x2 = _odd(x2, ln_mix[l], odd_w_in[o], sgu_ln_g[o], sgu_ln_b[o], sgu_w_s[o],
                      sgu_b_s[o], odd_w_out[o])
        x2 = _ffn(x2, ln_ffn2[l], ffn2_w_gate, ffn2_w_up, ffn2_w_down, l, mixer_out=mixer_out,
                  final_gain=final_norm if l == depth - 1 else None)
    return x2.reshape(b, s, d)
```

```python
import functools

import numpy as np
import jax
import jax.numpy as jnp
from jax import lax
from jax.experimental import pallas as pl
from jax.experimental.pallas import tpu as pltpu

F32 = jnp.float32
BF16 = jnp.bfloat16

EPS = 1e-6
GRID_W = 64
CHUNK = 128
ROPE_THETA = 10000.0
RET_HEADS = 4
RET_DIM = 128
RET_HEADS_PER_STEP = 4
ATT_Q_HEADS = 8
ATT_KV_HEADS = 2
ATT_DIM = 64
ATT_GROUP = ATT_Q_HEADS // ATT_KV_HEADS
ATT_Q_SCALE = ATT_DIM ** -0.5 * float(np.log2(np.e))
GMLP_GROUPS = 8

LANES = 128
BF16_SUBLANES = 16
VMEM_LIMIT_BYTES = 56 * 1024 * 1024

FFN_TM = 1024
STAGE_ROWS = 256
FFN_SUB = 256
EVEN_TM = 1024
EVEN_SUB = 256
ODD_TM = 1024
ODD_SUB = 256
ATT_TQ = 1024
ATT_UNIT_Q = 256
ATT_KEY_CHUNK = 256
MXU_N = 256


def _params(*sem):
    return pltpu.CompilerParams(dimension_semantics=sem, vmem_limit_bytes=VMEM_LIMIT_BYTES)


def _resident(shape):
    nd = len(shape)
    return pl.BlockSpec(shape, lambda *_: (0,) * nd, pipeline_mode=pl.Buffered(1))


def _rms(x):
    return x * lax.rsqrt(jnp.mean(x * x, axis=-1, keepdims=True) + EPS)


def _sigmoid(x):
    return 1.0 / (1.0 + jnp.exp(-x))


def _dot(a, b):
    return jnp.dot(a, b, preferred_element_type=F32)


def _col_chunks(n, step):
    return tuple((c, min(c + step, n)) for c in range(0, n, step))


def _stage_weights(pairs, stage_ref, sem):
    jobs = [(src, dst, r0) for src, dst in pairs for r0 in range(0, dst.shape[0], STAGE_ROWS)]

    def copy(j):
        src, dst, r0 = jobs[j]
        slot = j % 2
        return pltpu.make_async_copy(src.at[pl.ds(r0, STAGE_ROWS), :],
                                     stage_ref.at[slot, :, pl.ds(0, dst.shape[1])], sem.at[slot])

    copy(0).start()
    for j, (src, dst, r0) in enumerate(jobs):
        if j + 1 < len(jobs):
            copy(j + 1).start()
        copy(j).wait()
        dst[pl.ds(r0, STAGE_ROWS), :] = stage_ref[j % 2, :, :dst.shape[1]].astype(BF16)


def _ffn_kernel(*refs, layer, ff_chunks, mixer_out, final):
    wg_ref, wu_ref, wd_ref, stage_ref, sem = refs[-5:]
    x_ref, g_ref, wg_hbm, wu_hbm, wd_hbm = refs[:5]
    extra = list(refs[5:-6])
    o_ref = refs[-6]
    if mixer_out:
        r_ref, a_ref, wm_ref = extra[:3]
        extra = extra[3:]
        rw = r_ref.shape[1]
    if final:
        fg_ref, = extra

    @pl.when(pl.program_id(0) == 0)
    def _():
        _stage_weights([(wg_hbm.at[layer], wg_ref), (wu_hbm.at[layer], wu_ref),
                        (wd_hbm.at[layer], wd_ref)], stage_ref, sem)

    tm = x_ref.shape[0]
    sub = min(FFN_SUB, tm)
    if mixer_out:
        xm = x_ref[...] + _dot(r_ref[...], wm_ref[:rw, :]) + _dot(a_ref[...], wm_ref[rw:, :])
    for r0 in range(0, tm, sub):
        rows = slice(r0, r0 + sub)
        x = xm[rows, :] if mixer_out else x_ref[rows, :]
        hb = (_rms(x) * g_ref[...]).astype(BF16)
        acc = None
        for c0, c1 in ff_chunks:
            gate = _dot(hb, wg_ref[:, c0:c1])
            up = _dot(hb, wu_ref[:, c0:c1])
            a = (gate * _sigmoid(gate) * up).astype(BF16)
            d = _dot(a, wd_ref[c0:c1, :])
            acc = d if acc is None else acc + d
        y = x + 0.5 * acc
        if final:
            y = _rms(y) * fg_ref[...]
        o_ref[rows, :] = y


def _ffn(x2, gain, wg_all, wu_all, wd_all, layer, mixer_out=None, final_gain=None):
    t, d = x2.shape
    ff = wg_all.shape[2]
    tm = min(FFN_TM, t)
    assert d % STAGE_ROWS == 0 and ff % STAGE_ROWS == 0

    def row(w):
        return pl.BlockSpec((tm, w), lambda i: (i, 0))

    hbm = pl.BlockSpec(memory_space=pl.ANY)
    in_specs = [row(d), _resident((1, d)), hbm, hbm, hbm]
    args = [x2, gain.reshape(1, d), wg_all, wu_all, wd_all]
    if mixer_out is not None:
        ret2, att2, w_mix = mixer_out
        in_specs += [row(ret2.shape[1]), row(att2.shape[1]), _resident(w_mix.shape)]
        args += [ret2, att2, w_mix.astype(BF16)]
    if final_gain is not None:
        in_specs.append(_resident((1, d)))
        args.append(final_gain.reshape(1, d))
    name = "ffn" + ("_mix" if mixer_out is not None else "") + ("_final" if final_gain is not None else "")
    return pl.pallas_call(
        functools.partial(_ffn_kernel, layer=layer, ff_chunks=_col_chunks(ff, 4 * MXU_N),
                          mixer_out=mixer_out is not None, final=final_gain is not None),
        grid=(t // tm,),
        in_specs=in_specs,
        out_specs=row(d),
        out_shape=jax.ShapeDtypeStruct((t, d), F32),
        scratch_shapes=[pltpu.VMEM((d, ff), BF16), pltpu.VMEM((d, ff), BF16), pltpu.VMEM((ff, d), BF16),
                        pltpu.VMEM((2, STAGE_ROWS, max(d, ff)), F32), pltpu.SemaphoreType.DMA((2,))],
        compiler_params=_params("arbitrary"),
        name=name,
    )(*args)


def _rope(xb, cos, sin_signed, even_lane):
    nxt = pltpu.roll(xb, LANES - 1, 1)
    prv = pltpu.roll(xb, 1, 1)
    return xb * cos + jnp.where(even_lane, nxt, prv) * sin_signed


def _group_mean_sq(x, e_ref):
    sq = x * x
    hi = sq.astype(BF16)
    lo = (sq - hi.astype(F32)).astype(BF16)
    n = x.shape[1]
    w = min(MXU_N, n)
    e = e_ref[:w, :w]
    return jnp.concatenate([_dot(hi[:, c:c + w], e) + _dot(lo[:, c:c + w], e)
                            for c in range(0, n, w)], axis=1)


def _even_in_kernel(x_ref, g_ref, w_ref, e_ref, cr_ref, sr_ref, ca_ref, sa_ref, gq_ref, gk_ref,
                    rq_ref, rk_ref, rv_ref, rg_ref, aq_ref, ak_ref, avt_ref, *, splits):
    o_rq, o_rk, o_rv, o_rg, o_aq, o_ak, o_av, o_end = splits
    tm = x_ref.shape[0]
    sub = min(EVEN_SUB, tm)
    even_lane = (lax.broadcasted_iota(jnp.int32, (sub, LANES), 1) % 2) == 0
    for r0 in range(0, tm, sub):
        rows = slice(r0, r0 + sub)
        hb = (_rms(x_ref[rows, :]) * g_ref[...]).astype(BF16)
        cr, sr = cr_ref[rows, :], sr_ref[rows, :]
        ca, sa = ca_ref[rows, :], sa_ref[rows, :]

        def proj(c0, c1):
            return _dot(hb, w_ref[:, c0:c1])

        aq = proj(o_aq, o_ak)
        ak = proj(o_ak, o_av)
        rq = proj(o_rq, o_rk)
        rk = proj(o_rk, o_rv)
        aq = aq * lax.rsqrt(_group_mean_sq(aq, e_ref) + EPS) * gq_ref[...]
        for j in range(aq.shape[1] // LANES):
            sl = slice(j * LANES, (j + 1) * LANES)
            aq_ref[rows, sl] = (_rope(aq[:, sl], ca, sa, even_lane) * ATT_Q_SCALE).astype(BF16)
        ak = ak * lax.rsqrt(_group_mean_sq(ak, e_ref) + EPS) * gk_ref[...]
        ak = _rope(ak, ca, sa, even_lane).astype(BF16)
        avt = proj(o_av, o_end).T.astype(BF16)
        for h in range(ATT_KV_HEADS):
            sl = slice(h * ATT_DIM, (h + 1) * ATT_DIM)
            ak_ref[h, rows, :] = ak[:, sl]
            avt_ref[h, :, rows] = avt[sl, :]

        for h in range(RET_HEADS):
            sl = slice(h * RET_DIM, (h + 1) * RET_DIM)
            rq_ref[rows, sl] = _rope(rq[:, sl], cr, sr, even_lane).astype(BF16)
            rk_ref[rows, sl] = (_rope(rk[:, sl], cr, sr, even_lane) * RET_DIM ** -0.5).astype(BF16)
        rv_ref[rows, :] = proj(o_rv, o_rg).astype(BF16)
        rg_ref[rows, :] = proj(o_rg, o_aq)


def _rope_tables(seq, head_dim):
    rows = seq // GRID_W
    row = jnp.repeat(jnp.arange(rows), GRID_W).astype(F32)
    col = jnp.tile(jnp.arange(GRID_W), rows).astype(F32)
    axis_dim = head_dim // 2
    freqs = ROPE_THETA ** (-jnp.arange(0, axis_dim, 2, dtype=F32) / axis_dim)
    ang = jnp.concatenate([row[:, None] * freqs[None, :], col[:, None] * freqs[None, :]], axis=-1)
    cos = jnp.repeat(jnp.cos(ang), 2, axis=-1)
    sign = jnp.tile(jnp.array([-1.0, 1.0], F32), head_dim // 2)
    sin = jnp.repeat(jnp.sin(ang), 2, axis=-1) * sign[None, :]
    reps = LANES // head_dim
    return jnp.tile(cos, (1, reps)), jnp.tile(sin, (1, reps))


def _even_in(x3, gain, w_in, gq, gk):
    b, s, d = x3.shape
    tm = min(EVEN_TM, s)
    ret_w = RET_HEADS * RET_DIM
    att_w = ATT_Q_HEADS * ATT_DIM
    kv_w = ATT_KV_HEADS * ATT_DIM
    sizes = (ret_w, ret_w, ret_w, ret_w, att_w, kv_w, kv_w)
    splits = (0,) + tuple(int(v) for v in np.cumsum(sizes))
    n_in = splits[-1]
    assert w_in.shape == (d, n_in)
    cr, sr = _rope_tables(s, RET_DIM)
    ca, sa = _rope_tables(s, ATT_DIM)
    grp = np.arange(MXU_N) // ATT_DIM
    e = jnp.asarray((grp[:, None] == grp[None, :]).astype(np.float32) / ATT_DIM, BF16)
    gq_t = jnp.tile(gq, ATT_Q_HEADS).reshape(1, att_w)
    gk_t = jnp.tile(gk, ATT_KV_HEADS).reshape(1, kv_w)

    def row(w):
        return pl.BlockSpec((None, tm, w), lambda bi, si: (bi, si, 0))

    tab = pl.BlockSpec((tm, LANES), lambda bi, si: (si, 0))
    k_out = pl.BlockSpec((None, ATT_KV_HEADS, tm, ATT_DIM), lambda bi, si: (bi, 0, si, 0))
    vt_out = pl.BlockSpec((None, ATT_KV_HEADS, ATT_DIM, tm), lambda bi, si: (bi, 0, 0, si))
    return pl.pallas_call(
        functools.partial(_even_in_kernel, splits=splits),
        grid=(b, s // tm),
        in_specs=[row(d), _resident((1, d)), _resident((d, n_in)), _resident((MXU_N, MXU_N)),
                  tab, tab, tab, tab, _resident((1, att_w)), _resident((1, kv_w))],
        out_specs=[row(ret_w), row(ret_w), row(ret_w), row(ret_w), row(att_w), k_out, vt_out],
        out_shape=[jax.ShapeDtypeStruct((b, s, ret_w), BF16),
                   jax.ShapeDtypeStruct((b, s, ret_w), BF16),
                   jax.ShapeDtypeStruct((b, s, ret_w), BF16),
                   jax.ShapeDtypeStruct((b, s, ret_w), F32),
                   jax.ShapeDtypeStruct((b, s, att_w), BF16),
                   jax.ShapeDtypeStruct((b, ATT_KV_HEADS, s, ATT_DIM), BF16),
                   jax.ShapeDtypeStruct((b, ATT_KV_HEADS, ATT_DIM, s), BF16)],
        compiler_params=_params("parallel", "parallel"),
        name="even_in",
    )(x3, gain.reshape(1, d), w_in.astype(BF16), e, cr, sr, ca, sa, gq_t, gk_t)


def _ret_kernel(df_ref, db_ref, q_ref, k_ref, v_ref, g_ref, o_ref, st_ref):
    c = CHUNK
    d = RET_DIM
    n_chunks = q_ref.shape[0] // c
    heads = range(q_ref.shape[1] // d)
    ri = lax.broadcasted_iota(jnp.int32, (c, c), 0).astype(F32)
    ci = lax.broadcasted_iota(jnp.int32, (c, c), 1).astype(F32)
    diff = ri - ci
    tdot = functools.partial(lax.dot_general, dimension_numbers=(((0,), (0,)), ((), ())),
                             preferred_element_type=F32)
    mask, kw_f, qw_f, kw_b, qw_b, dec_f, dec_b = [], [], [], [], [], [], []
    for h in heads:
        lam_f = jnp.exp(df_ref[h])
        lam_b = jnp.exp(db_ref[h])
        mask.append(jnp.where(diff >= 0,
                              jnp.exp(-lam_f * jnp.maximum(diff, 0.0)),
                              jnp.exp(-lam_b * jnp.maximum(-diff, 0.0))))
        kw_f.append(jnp.exp(-lam_f * (c - 1.0 - ri)))
        qw_f.append(jnp.exp(-lam_f * (ri + 1.0)))
        kw_b.append(jnp.exp(-lam_b * ri))
        qw_b.append(jnp.exp(-lam_b * (c - ri)))
        dec_f.append(jnp.exp(-lam_f * c))
        dec_b.append(jnp.exp(-lam_b * c))

    def blk(ref, n, h):
        return ref[n * c:(n + 1) * c, h * d:(h + 1) * d]

    st = [jnp.zeros((c, c), F32) for _ in heads]
    for n in range(n_chunks):
        for h in heads:
            st_ref[h, n, :c, :] = st[h].astype(BF16)
            k = blk(k_ref, n, h).astype(F32)
            st[h] = dec_f[h] * st[h] + tdot((k * kw_f[h]).astype(BF16), blk(v_ref, n, h))
    st = [jnp.zeros((c, c), F32) for _ in heads]
    for n in reversed(range(n_chunks)):
        for h in heads:
            st_ref[h, n, c:, :] = st[h].astype(BF16)
            k = blk(k_ref, n, h).astype(F32)
            st[h] = dec_b[h] * st[h] + tdot((k * kw_b[h]).astype(BF16), blk(v_ref, n, h))

    for n in range(n_chunks):
        sc = [lax.dot_general(blk(q_ref, n, h), blk(k_ref, n, h), (((1,), (1,)), ((), ())),
                              preferred_element_type=F32) for h in heads]
        cross = []
        for h in heads:
            q = blk(q_ref, n, h).astype(F32)
            qq = jnp.concatenate([(q * qw_f[h]).astype(BF16), (q * qw_b[h]).astype(BF16)], axis=1)
            cross.append(_dot(qq, st_ref[h, n]))
        for h in heads:
            o = _dot((sc[h] * mask[h]).astype(BF16), blk(v_ref, n, h)) + cross[h]
            gate = blk(g_ref, n, h)
            o_ref[n * c:(n + 1) * c, h * d:(h + 1) * d] = (
                _rms(o) * (gate * _sigmoid(gate))).astype(BF16)


def _retention(rq, rk, rv, rg, decay_f, decay_b):
    b, s, w = rq.shape
    heads = w // RET_DIM
    hp = RET_HEADS_PER_STEP
    n_chunks = s // CHUNK
    blk = pl.BlockSpec((None, s, hp * RET_DIM), lambda bi, hi: (bi, 0, hi))
    dec = pl.BlockSpec((hp, 1, LANES), lambda bi, hi: (hi, 0, 0))
    df = jnp.broadcast_to(decay_f.reshape(heads, 1, 1), (heads, 1, LANES))
    db = jnp.broadcast_to(decay_b.reshape(heads, 1, 1), (heads, 1, LANES))
    return pl.pallas_call(
        _ret_kernel,
        grid=(b, heads // hp),
        in_specs=[dec, dec, blk, blk, blk, blk],
        out_specs=blk,
        out_shape=jax.ShapeDtypeStruct((b, s, w), BF16),
        scratch_shapes=[pltpu.VMEM((hp, n_chunks, 2 * CHUNK, RET_DIM), BF16)],
        compiler_params=_params("parallel", "parallel"),
        name="retention",
    )(df, db, rq, rk, rv, rg)


def _att_kernel(q_ref, k_ref, vt_ref, o_ref, s_ref, m_ref):
    i = pl.program_id(0)
    d = k_ref.shape[1]
    n_units, n_keys, uq = s_ref.shape
    units = [(r, h) for r in range(n_units // ATT_GROUP) for h in range(ATT_GROUP)]
    kc = min(ATT_KEY_CHUNK, n_keys)

    @pl.when(i == 0)
    def _():
        s_ref[...] = jnp.zeros(s_ref.shape, F32)
        m_ref[...] = jnp.zeros(m_ref.shape, F32)

    ones = jnp.ones((BF16_SUBLANES, kc), BF16)
    outs = []
    for u0 in range(0, n_units, 2):
        pair = (u0, u0 + 1)
        qs = {u: q_ref[units[u][0] * uq:(units[u][0] + 1) * uq,
                       units[u][1] * d:(units[u][1] + 1) * d] for u in pair}
        prev_max = {u: m_ref[u] for u in pair}
        mx = {u: None for u in pair}
        acc = {u: None for u in pair}
        for c0 in range(0, n_keys, kc):
            rows = slice(c0, c0 + kc)
            vte = jnp.concatenate([vt_ref[:, rows], ones], axis=0)
            for u in pair:
                p = jnp.exp2(s_ref[u, rows, :] - prev_max[u])
                pv = _dot(vte, p.astype(BF16))
                acc[u] = pv if acc[u] is None else acc[u] + pv
            for u in pair:
                st = lax.dot_general(k_ref[rows, :], qs[u], (((1,), (1,)), ((), ())),
                                     preferred_element_type=F32)
                s_ref[u, rows, :] = st
                cm = jnp.max(st, axis=0, keepdims=True)
                mx[u] = cm if mx[u] is None else jnp.maximum(mx[u], cm)
        for u in pair:
            m_ref[u] = mx[u]
            outs.append(acc[u][:d, :] * (1.0 / acc[u][d:d + 1, :]))
        if len(outs) == ATT_GROUP:
            r = units[u0][0]
            o_ref[r * uq:(r + 1) * uq, :] = jnp.concatenate(outs, axis=0).T.astype(BF16)
            outs = []


def _attention(aq, ak, avt):
    b, s, w = aq.shape
    tq = min(ATT_TQ, s)
    uq = min(ATT_UNIT_Q, tq)
    gw = ATT_GROUP * ATT_DIM
    nq = s // tq
    n_tiles = b * ATT_KV_HEADS * nq
    n_units = ATT_GROUP * (tq // uq)

    def tile(t):
        return t // (ATT_KV_HEADS * nq), (t // nq) % ATT_KV_HEADS, t % nq

    def score_tile(i):
        return tile(jnp.minimum(i, n_tiles - 1))

    def softmax_tile(i):
        return tile(jnp.maximum(i - 1, 0))

    def qmap(bgq):
        bi, gi, qi = bgq
        return bi, qi, gi

    def kvmap(bgq):
        bi, gi, _ = bgq
        return bi, gi, 0, 0

    return pl.pallas_call(
        _att_kernel,
        grid=(n_tiles + 1,),
        in_specs=[pl.BlockSpec((None, tq, gw), lambda i: qmap(score_tile(i))),
                  pl.BlockSpec((None, None, s, ATT_DIM), lambda i: kvmap(score_tile(i))),
                  pl.BlockSpec((None, None, ATT_DIM, s), lambda i: kvmap(softmax_tile(i)))],
        out_specs=pl.BlockSpec((None, tq, gw), lambda i: qmap(softmax_tile(i))),
        out_shape=jax.ShapeDtypeStruct((b, s, w), BF16),
        scratch_shapes=[pltpu.VMEM((n_units, s, uq), F32), pltpu.VMEM((n_units, 1, uq), F32)],
        compiler_params=_params("arbitrary"),
        name="attention",
    )(aq, ak, avt)


def _gelu2(x):
    return x + x * lax.erf(x * (2.0 ** -0.5))


def _spatial_mix(w, bias, vg):
    n = vg.shape[0] // CHUNK
    gw = vg.shape[1]
    wide = _dot(w, jnp.concatenate([vg[i * CHUNK:(i + 1) * CHUNK, :] for i in range(n)], axis=1))
    return jnp.concatenate([wide[:, i * gw:(i + 1) * gw] + bias for i in range(n)], axis=0)


def _odd_kernel(x_ref, g_ref, wi_ref, lg_ref, lb_ref, ws_ref, bs_ref, wo_ref, o_ref):
    half = wo_ref.shape[0]
    gw = half // GMLP_GROUPS
    pair = 2 * gw
    tm = x_ref.shape[0]
    sub = min(ODD_SUB, tm)
    blocks = [slice(r0, r0 + sub) for r0 in range(0, tm, sub)]

    hbs, vns = [], []
    for rows in blocks:
        hb = (_rms(x_ref[rows, :]) * g_ref[...]).astype(BF16)
        v2 = _gelu2(_dot(hb, wi_ref[:, half:]))
        mu = jnp.mean(v2, axis=-1, keepdims=True)
        vc = v2 - mu
        var = jnp.mean(vc * vc, axis=-1, keepdims=True)
        hbs.append(hb)
        vns.append((vc * lax.rsqrt(var + 4.0 * EPS) * lg_ref[...] + lb_ref[...]).astype(BF16))

    for rows, hb, vn in zip(blocks, hbs, vns):
        gated = []
        for c0 in range(0, half, pair):
            u = _gelu2(_dot(hb, wi_ref[:, c0:c0 + pair]))
            mixed = jnp.concatenate([_spatial_mix(ws_ref[g], bs_ref[g], vn[:, g * gw:(g + 1) * gw])
                                     for g in (c0 // gw, c0 // gw + 1)], axis=1)
            gated.append((u * mixed).astype(BF16))
        o_ref[rows, :] = x_ref[rows, :] + _dot(jnp.concatenate(gated, axis=1), wo_ref[...])


def _odd(x2, gain, w_in, ln_g, ln_b, w_s, b_s, w_out):
    t, d = x2.shape
    tm = min(ODD_TM, t)
    half = w_out.shape[0]
    gw = half // GMLP_GROUPS
    assert w_in.shape == (d, 2 * half) and w_s.shape == (GMLP_GROUPS, CHUNK, CHUNK)
    bs_full = jnp.broadcast_to(b_s[:, :, None], (GMLP_GROUPS, CHUNK, gw))
    row = pl.BlockSpec((tm, d), lambda i: (i, 0))
    return pl.pallas_call(
        _odd_kernel,
        grid=(t // tm,),
        in_specs=[row, _resident((1, d)), _resident((d, 2 * half)), _resident((1, half)),
                  _resident((1, half)), _resident((GMLP_GROUPS, CHUNK, CHUNK)),
                  _resident((GMLP_GROUPS, CHUNK, gw)), _resident((half, d))],
        out_specs=row,
        out_shape=jax.ShapeDtypeStruct((t, d), F32),
        compiler_params=_params("parallel"),
        name="odd_mixer",
    )(x2, gain.reshape(1, d), w_in.astype(BF16), ln_g.reshape(1, half), ln_b.reshape(1, half),
      w_s.astype(BF16), bs_full, (0.5 * w_out).astype(BF16))


def kernel(x, ln_ffn1, ffn1_w_gate, ffn1_w_up, ffn1_w_down, ln_mix, even_w_in, ret_decay_fwd, ret_decay_bwd, att_q_norm, att_k_norm, even_w_out, odd_w_in, sgu_ln_g, sgu_ln_b, sgu_w_s, sgu_b_s, odd_w_out, ln_ffn2, ffn2_w_gate, ffn2_w_up, ffn2_w_down, final_norm):
    b, s, d = x.shape
    depth = ln_ffn1.shape[0]
    t = b * s
    x2 = x.reshape(t, d)
    for l in range(depth):
        x2 = _ffn(x2, ln_ffn1[l], ffn1_w_gate, ffn1_w_up, ffn1_w_down, l)
        mixer_out = None
        if l % 2 == 0:
            e = l // 2
            rq, rk, rv, rg, aq, ak, avt = _even_in(x2.reshape(b, s, d), ln_mix[l], even_w_in[e],
                                                   att_q_norm[e], att_k_norm[e])
            ret = _retention(rq, rk, rv, rg, ret_decay_fwd[e], ret_decay_bwd[e])
            att = _attention(aq, ak, avt)
            mixer_out = (ret.reshape(t, -1), att.reshape(t, -1), even_w_out[e])
        else:
            o = l // 2
            x2 = _odd(x2, ln_mix[l], odd_w_in[o], sgu_ln_g[o], sgu_ln_b[o], sgu_w_s[o],
                      sgu_b_s[o], odd_w_out[o])
        x2 = _ffn(x2, ln_ffn2[l], ffn2_w_gate, ffn2_w_up, ffn2_w_down, l, mixer_out=mixer_out,
                  final_gain=final_norm if l == depth - 1 else None)
    return x2.reshape(b, s, d)
```
